```python
import jax, jax.numpy as jnp
from jax import lax
import numpy as np

D_MODEL = 1024
BATCH = 2
SEQ = 8192
DEPTH = 2

D_MIX = D_MODEL
N_MIXERS = 4
D_GROUP = D_MIX // N_MIXERS
HEADS_PER_MIXER = 4
HEAD_DIM = D_GROUP // HEADS_PER_MIXER
POOL_WINDOWS = (2, 4, 8, 16)
CONV_WIDTH = 3
CHUNK = 128
D_FF = 2816
D_IN = D_GROUP * 7
SPLITS = tuple(D_GROUP * i for i in (1, 2, 3, 4, 5, 6))
ALPHA = (2 * DEPTH) ** 0.25
BETA = (8 * DEPTH) ** -0.25
LN_EPS = 1e-5

kernel_name = "hybrid_headgroup_fourier_pool_conv_gmlp_encoder"


def layer_norm(x, g, b):
    xf = x.astype(jnp.float32)
    mu = jnp.mean(xf, axis=-1, keepdims=True)
    var = jnp.mean(jnp.square(xf - mu), axis=-1, keepdims=True)
    return ((xf - mu) * lax.rsqrt(var + LN_EPS) * g + b).astype(x.dtype)


def swiglu_ffn(x, w_gate, w_up, w_down):
    return (jax.nn.silu(x @ w_gate) * (x @ w_up)) @ w_down


def fourier_mixer(a):
    bsz, s, _ = a.shape
    ah = a.astype(jnp.float32).reshape(bsz, s, HEADS_PER_MIXER, HEAD_DIM)
    f = jnp.fft.fftn(ah, axes=(1, 3), norm="ortho").real
    return f.reshape(bsz, s, D_GROUP).astype(a.dtype)


def pool_mixer(p, pool_w, pool_scale):
    bsz, s, _ = p.shape
    pf = p.astype(jnp.float32)
    cs = jnp.concatenate([jnp.zeros((bsz, 1, D_GROUP), jnp.float32), jnp.cumsum(pf, axis=1)], axis=1)
    t = jnp.arange(s)
    outs = []
    for g, w in enumerate(POOL_WINDOWS):
        lo = jnp.clip(t - w // 2, 0, s)
        hi = jnp.clip(t + w - w // 2, 0, s)
        sl = slice(g * HEAD_DIM, (g + 1) * HEAD_DIM)
        c = cs[..., sl]
        win_sum = jnp.take(c, hi, axis=1) - jnp.take(c, lo, axis=1)
        mean = win_sum / (hi - lo).astype(jnp.float32)[None, :, None]
        outs.append(mean - pf[..., sl])
    pooled = jnp.concatenate(outs, axis=-1).astype(p.dtype).reshape(bsz, s, HEADS_PER_MIXER, HEAD_DIM)
    mixed = jnp.einsum('bsgc,gcd->bsgd', pooled, pool_w).reshape(bsz, s, D_GROUP)
    return mixed * pool_scale


def conv_mixer(gate_b, gate_c, h, conv_w):
    s = h.shape[1]
    z = gate_c * h
    pad = CONV_WIDTH // 2
    zp = jnp.pad(z, ((0, 0), (pad, CONV_WIDTH - 1 - pad), (0, 0)))
    y = conv_w[0] * zp[:, 0:s]
    for k in range(1, CONV_WIDTH):
        y = y + conv_w[k] * zp[:, k:k + s]
    return gate_b * y


def gmlp_mixer(u, v, ln_g, ln_b, w_s, b_s):
    bsz, s, _ = v.shape
    vn = layer_norm(v, ln_g, ln_b).reshape(bsz, s // CHUNK, CHUNK, HEADS_PER_MIXER, HEAD_DIM)
    mixed = jnp.einsum('hqk,bnkhd->bnqhd', w_s, vn) + b_s.T[None, None, :, :, None]
    return u * mixed.reshape(bsz, s, D_GROUP)


def headgroup_mixer(x, w_in, pool_w, pool_scale, conv_w, gmlp_ln_g, gmlp_ln_b, gmlp_w_s, gmlp_b_s, w_out):
    proj = x @ w_in
    a, p, gate_b, gate_c, h, u, v = jnp.split(proj, SPLITS, axis=-1)
    y = jnp.concatenate([
        fourier_mixer(a),
        pool_mixer(p, pool_w, pool_scale),
        conv_mixer(gate_b, gate_c, h, conv_w),
        gmlp_mixer(u, v, gmlp_ln_g, gmlp_ln_b, gmlp_w_s, gmlp_b_s),
    ], axis=-1)
    return y @ w_out


def setup_inputs(seed: int = 0) -> dict:
    key = jax.random.key(seed)
    ks = jax.random.split(key, 24)
    L = DEPTH

    def nrm(k, shape, scale):
        return jax.random.normal(k, shape, jnp.float32) * scale

    return {
        "x": nrm(ks[0], (BATCH, SEQ, D_MODEL), 1.0),
        "ffn1_w_gate": nrm(ks[1], (L, D_MODEL, D_FF), D_MODEL ** -0.5),
        "ffn1_w_up": nrm(ks[2], (L, D_MODEL, D_FF), D_MODEL ** -0.5),
        "ffn1_w_down": nrm(ks[3], (L, D_FF, D_MODEL), BETA * D_FF ** -0.5),
        "ln1_g": 1.0 + nrm(ks[4], (L, D_MODEL), 0.05),
        "ln1_b": nrm(ks[5], (L, D_MODEL), 0.02),
        "mix_w_in": nrm(ks[6], (L, D_MODEL, D_IN), D_MODEL ** -0.5),
        "pool_w": nrm(ks[7], (L, HEADS_PER_MIXER, HEAD_DIM, HEAD_DIM), HEAD_DIM ** -0.5),
        "pool_scale": 1.0 + nrm(ks[8], (L, D_GROUP), 0.1),
        "conv_w": nrm(ks[9], (L, CONV_WIDTH, D_GROUP), CONV_WIDTH ** -0.5),
        "gmlp_ln_g": 1.0 + nrm(ks[10], (L, D_GROUP), 0.05),
        "gmlp_ln_b": nrm(ks[11], (L, D_GROUP), 0.02),
        "gmlp_w_s": nrm(ks[12], (L, HEADS_PER_MIXER, CHUNK, CHUNK), CHUNK ** -0.5),
        "gmlp_b_s": 1.0 + nrm(ks[13], (L, HEADS_PER_MIXER, CHUNK), 0.02),
        "mix_w_out": nrm(ks[14], (L, D_MIX, D_MODEL), BETA * D_MIX ** -0.5),
        "ln2_g": 1.0 + nrm(ks[15], (L, D_MODEL), 0.05),
        "ln2_b": nrm(ks[16], (L, D_MODEL), 0.02),
        "ffn2_w_gate": nrm(ks[17], (L, D_MODEL, D_FF), D_MODEL ** -0.5),
        "ffn2_w_up": nrm(ks[18], (L, D_MODEL, D_FF), D_MODEL ** -0.5),
        "ffn2_w_down": nrm(ks[19], (L, D_FF, D_MODEL), BETA * D_FF ** -0.5),
        "ln3_g": 1.0 + nrm(ks[20], (L, D_MODEL), 0.05),
        "ln3_b": nrm(ks[21], (L, D_MODEL), 0.02),
    }


def reference(x, ffn1_w_gate, ffn1_w_up, ffn1_w_down, ln1_g, ln1_b,
              mix_w_in, pool_w, pool_scale, conv_w, gmlp_ln_g, gmlp_ln_b, gmlp_w_s, gmlp_b_s,
              mix_w_out, ln2_g, ln2_b,
              ffn2_w_gate, ffn2_w_up, ffn2_w_down, ln3_g, ln3_b):
    for l in range(DEPTH):
        x = layer_norm(ALPHA * x + 0.5 * swiglu_ffn(x, ffn1_w_gate[l], ffn1_w_up[l], ffn1_w_down[l]),
                       ln1_g[l], ln1_b[l])
        mix = headgroup_mixer(x, mix_w_in[l], pool_w[l], pool_scale[l], conv_w[l],
                              gmlp_ln_g[l], gmlp_ln_b[l], gmlp_w_s[l], gmlp_b_s[l], mix_w_out[l])
        x = layer_norm(ALPHA * x + mix, ln2_g[l], ln2_b[l])
        x = layer_norm(ALPHA * x + 0.5 * swiglu_ffn(x, ffn2_w_gate[l], ffn2_w_up[l], ffn2_w_down[l]),
                       ln3_g[l], ln3_b[l])
    return x
```

```python
import functools

import numpy as np
import jax
import jax.numpy as jnp
from jax import lax
from jax.experimental import pallas as pl
from jax.experimental.pallas import tpu as pltpu

F32 = jnp.float32
BF16 = jnp.bfloat16

D_GROUP = 256
HEADS = 4
HEAD_DIM = 64
N_PROJ_GROUPS = 7
POOL_WINDOWS = (2, 4, 8, 16)
CHUNK = 128
LN_EPS = 1e-5
FF_CHUNK = 256
HALO = 8
DFT_N1 = 64
DFT_N2 = 128

V7X_VMEM_LIMIT_BYTES = 56 * 1024 * 1024


def _resident(shape):
    return pl.BlockSpec(shape, lambda *_: (0,) * len(shape), pipeline_mode=pl.Buffered(1))


def _layer_norm(y, g, b):
    mu = jnp.mean(y, axis=-1, keepdims=True)
    yc = y - mu
    var = jnp.mean(yc * yc, axis=-1, keepdims=True)
    return yc * lax.rsqrt(var + LN_EPS) * g + b


def _swiglu(xb, wg_ref, wu_ref, wd_ref, acc_ref):
    d_ff = wg_ref.shape[1]
    for c in range(d_ff // FF_CHUNK):
        sl = slice(c * FF_CHUNK, (c + 1) * FF_CHUNK)
        g = jnp.dot(xb, wg_ref[:, sl], preferred_element_type=F32)
        u = jnp.dot(xb, wu_ref[:, sl], preferred_element_type=F32)
        h = (g * jax.nn.sigmoid(g) * u).astype(BF16)
        d = jnp.dot(h, wd_ref[sl, :], preferred_element_type=F32)
        if c == 0:
            acc_ref[...] = d
        else:
            acc_ref[...] += d
    return acc_ref[...]


def _ffn_in_body(x_ref, wg_ref, wu_ref, wd_ref, g_ref, b_ref, win_ref,
                 x1_ref, proj_ref, acc_ref, *, alpha):
    x = x_ref[...]
    ffn = _swiglu(x.astype(BF16), wg_ref, wu_ref, wd_ref, acc_ref)
    x1 = _layer_norm(alpha * x + 0.5 * ffn, g_ref[...], b_ref[...])
    x1_ref[...] = x1
    x1b = x1.astype(BF16)
    for j in range(N_PROJ_GROUPS):
        sl = slice(j * D_GROUP, (j + 1) * D_GROUP)
        proj_ref[j] = jnp.dot(x1b, win_ref[:, sl], preferred_element_type=F32)


def _ffn_in(x, wg, wu, wd, g, b, w_in, *, alpha, tm):
    n, d = x.shape
    d_ff = wg.shape[1]
    row = lambda i: (i, 0)
    return pl.pallas_call(
        functools.partial(_ffn_in_body, alpha=alpha),
        grid=(n // tm,),
        in_specs=[
            pl.BlockSpec((tm, d), row),
            _resident((d, d_ff)), _resident((d, d_ff)), _resident((d_ff, d)),
            _resident((1, d)), _resident((1, d)),
            _resident(w_in.shape),
        ],
        out_specs=[
            pl.BlockSpec((tm, d), row),
            pl.BlockSpec((N_PROJ_GROUPS, tm, D_GROUP), lambda i: (0, i, 0)),
        ],
        out_shape=[
            jax.ShapeDtypeStruct((n, d), F32),
            jax.ShapeDtypeStruct((N_PROJ_GROUPS, n, D_GROUP), F32),
        ],
        scratch_shapes=[pltpu.VMEM((tm, d), F32)],
        compiler_params=pltpu.CompilerParams(
            dimension_semantics=("arbitrary",), vmem_limit_bytes=V7X_VMEM_LIMIT_BYTES),
        name="ffn_in",
    )(x, wg, wu, wd, g, b, w_in)


def _dft_tables(seq):
    n1, n2 = DFT_N1, DFT_N2
    assert n1 * n2 == seq
    c = np.arange(HEAD_DIM)
    ang = 2.0 * np.pi * ((c[:, None] * c[None, :]) % HEAD_DIM) / HEAD_DIM
    eye = np.eye(HEADS)
    wc = np.concatenate([np.kron(eye, np.cos(ang)), -np.kron(eye, np.sin(ang))], axis=1)
    k = np.arange(n1)
    ang1 = 2.0 * np.pi * ((k[:, None] * k[None, :]) % n1) / n1
    fr, fi = np.cos(ang1), -np.sin(ang1)
    f1 = np.block([[fr, -fi], [fi, fr]])
    k1 = np.arange(n1)[:, None, None]
    k2 = np.arange(n2)[None, :, None]
    m = np.arange(n2)[None, None, :]
    ang2 = 2.0 * np.pi * ((m * (k1 + n1 * k2)) % seq) / seq
    scale = 1.0 / np.sqrt(float(seq) * HEAD_DIM)
    g2 = np.concatenate([np.cos(ang2), np.sin(ang2)], axis=2) * scale
    return wc.astype(np.float32), f1.astype(np.float32), g2.astype(np.float32)


def _fourier_a_body(a_ref, wc_ref, f1_ref, z_ref, u_ref, *, t):
    n1 = DFT_N1
    for j in range(t):
        sl = slice(j * D_GROUP, (j + 1) * D_GROUP)
        uc = jnp.dot(a_ref[:, sl].astype(BF16), wc_ref[...], preferred_element_type=F32)
        u_ref[0:n1, sl] = uc[:, :D_GROUP].astype(BF16)
        u_ref[n1:2 * n1, sl] = uc[:, D_GROUP:].astype(BF16)
    z = jnp.dot(f1_ref[...], u_ref[...], preferred_element_type=F32)
    z_ref[0] = z[:n1].astype(BF16)
    z_ref[1] = z[n1:].astype(BF16)


def _fourier_b_body(z_ref, g2_ref, y_ref, *, kt):
    for k in range(kt):
        zs = jnp.concatenate([z_ref[0, k], z_ref[1, k]], axis=0)
        r = jnp.dot(g2_ref[k], zs, preferred_element_type=F32)
        y_ref[:, k * D_GROUP:(k + 1) * D_GROUP] = r.astype(BF16)


def _fourier(proj, wc, f1, g2, *, bsz, seq, t=8, kt=8):
    n1, n2 = DFT_N1, DFT_N2
    a = proj.reshape(N_PROJ_GROUPS, bsz, n1, n2 * D_GROUP)
    z = pl.pallas_call(
        functools.partial(_fourier_a_body, t=t),
        grid=(bsz, n2 // t),
        in_specs=[
            pl.BlockSpec((None, None, n1, t * D_GROUP), lambda b, i: (0, b, 0, i)),
            _resident(wc.shape), _resident(f1.shape),
        ],
        out_specs=pl.BlockSpec((None, 2, n1, t * D_GROUP), lambda b, i: (b, 0, 0, i)),
        out_shape=jax.ShapeDtypeStruct((bsz, 2, n1, n2 * D_GROUP), BF16),
        scratch_shapes=[pltpu.VMEM((2 * n1, t * D_GROUP), BF16)],
        compiler_params=pltpu.CompilerParams(
            dimension_semantics=("arbitrary", "arbitrary"),
            vmem_limit_bytes=V7X_VMEM_LIMIT_BYTES),
        name="fourier_a",
    )(a, wc, f1)
    z = z.reshape(bsz, 2, n1, n2, D_GROUP)
    y = pl.pallas_call(
        functools.partial(_fourier_b_body, kt=kt),
        grid=(bsz, n1 // kt),
        in_specs=[
            pl.BlockSpec((None, 2, kt, n2, D_GROUP), lambda b, i: (b, 0, i, 0, 0)),
            pl.BlockSpec((kt, n2, 2 * n2), lambda b, i: (i, 0, 0)),
        ],
        out_specs=pl.BlockSpec((None, n2, kt * D_GROUP), lambda b, i: (b, 0, i)),
        out_shape=jax.ShapeDtypeStruct((bsz, n2, n1 * D_GROUP), BF16),
        compiler_params=pltpu.CompilerParams(
            dimension_semantics=("arbitrary", "arbitrary"),
            vmem_limit_bytes=V7X_VMEM_LIMIT_BYTES),
        name="fourier_b",
    )(z, g2)
    return y.reshape(bsz * seq, D_GROUP)


def _shift_rows(x, d):
    return pltpu.roll(x, d % x.shape[0], axis=0)


def _local_body(p_ref, gb_ref, gc_ref, h_ref, u_ref, v_ref,
                pp_ref, pn_ref, gcp_ref, gcn_ref, hp_ref, hn_ref,
                poolw_ref, pscale_ref, convw_ref, lng_ref, lnb_ref, ws_ref, bs_ref,
                y_ref, *, tl, tiles_per_seq, seq):
    ti = lax.rem(pl.program_id(0), tiles_per_seq)
    keep_prev = (ti > 0).astype(F32)
    keep_next = (ti < tiles_per_seq - 1).astype(F32)
    core = slice(HALO, HALO + tl)
    lane = lax.broadcasted_iota(jnp.int32, (1, D_GROUP), 1)
    head = lane // HEAD_DIM

    p = p_ref[...]
    x = jnp.concatenate([pp_ref[...] * keep_prev, p, pn_ref[...] * keep_next], axis=0)
    w2 = x + _shift_rows(x, 1)
    w4 = _shift_rows(w2, 1) + _shift_rows(w2, -1)
    w8 = _shift_rows(w4, 2) + _shift_rows(w4, -2)
    w16 = _shift_rows(w8, 4) + _shift_rows(w8, -4)
    win = jnp.where(head == 0, w2[core],
                    jnp.where(head == 1, w4[core],
                              jnp.where(head == 2, w8[core], w16[core])))
    half = jnp.where(head == 0, POOL_WINDOWS[0] // 2,
                     jnp.where(head == 1, POOL_WINDOWS[1] // 2,
                               jnp.where(head == 2, POOL_WINDOWS[2] // 2,
                                         POOL_WINDOWS[3] // 2)))
    t = ti * tl + lax.broadcasted_iota(jnp.int32, (tl, D_GROUP), 0)
    cnt = jnp.minimum(t + half, seq) - jnp.maximum(t - half, 0)
    pooled = win / cnt.astype(F32) - p
    mixed = jnp.dot(pooled.astype(BF16), poolw_ref[...], preferred_element_type=F32)
    y_ref[:, 0:D_GROUP] = (mixed * pscale_ref[...]).astype(BF16)

    z = jnp.concatenate([gcp_ref[...] * hp_ref[...] * keep_prev,
                         gc_ref[...] * h_ref[...],
                         gcn_ref[...] * hn_ref[...] * keep_next], axis=0)
    conv = (convw_ref[0:1, :] * _shift_rows(z, 1) + convw_ref[1:2, :] * z
            + convw_ref[2:3, :] * _shift_rows(z, -1))
    y_ref[:, D_GROUP:2 * D_GROUP] = (gb_ref[...] * conv[core]).astype(BF16)

    vn = _layer_norm(v_ref[...], lng_ref[...], lnb_ref[...])
    for c in range(tl // CHUNK):
        rows = slice(c * CHUNK, (c + 1) * CHUNK)
        vc = vn[rows]
        stacked = jnp.concatenate(
            [jnp.where(head == hd, vc, 0.0).astype(BF16) for hd in range(HEADS)], axis=0)
        mix = jnp.dot(ws_ref[...], stacked, preferred_element_type=F32) + bs_ref[...]
        y_ref[rows, 2 * D_GROUP:3 * D_GROUP] = (u_ref[rows, :] * mix).astype(BF16)


def _local(proj, pool_bd, pool_scale, conv_w, ln_g, ln_b, ws_cat, bs_full, *, seq, tl):
    n = proj.shape[1]
    tiles_per_seq = seq // tl
    hb = tl // HALO
    last_hb = n // HALO - 1

    def main(g):
        return pl.BlockSpec((None, tl, D_GROUP), lambda i: (g, i, 0))

    def prev(g):
        return pl.BlockSpec((None, HALO, D_GROUP),
                            lambda i: (g, jnp.maximum(i * hb - 1, 0), 0))

    def nxt(g):
        return pl.BlockSpec((None, HALO, D_GROUP),
                            lambda i: (g, jnp.minimum((i + 1) * hb, last_hb), 0))

    return pl.pallas_call(
        functools.partial(_local_body, tl=tl, tiles_per_seq=tiles_per_seq, seq=seq),
        grid=(n // tl,),
        in_specs=[main(1), main(2), main(3), main(4), main(5), main(6),
                  prev(1), nxt(1), prev(3), nxt(3), prev(4), nxt(4),
                  _resident(pool_bd.shape), _resident(pool_scale.shape),
                  _resident(conv_w.shape), _resident(ln_g.shape), _resident(ln_b.shape),
                  _resident(ws_cat.shape), _resident(bs_full.shape)],
        out_specs=pl.BlockSpec((tl, 3 * D_GROUP), lambda i: (i, 0)),
        out_shape=jax.ShapeDtypeStruct((n, 3 * D_GROUP), BF16),
        compiler_params=pltpu.CompilerParams(
            dimension_semantics=("arbitrary",), vmem_limit_bytes=V7X_VMEM_LIMIT_BYTES),
        name="local_mixers",
    )(*([proj] * 12), pool_bd, pool_scale, conv_w, ln_g, ln_b, ws_cat, bs_full)


def _mix_ffn_body(x1_ref, ya_ref, yl_ref, wo_ref, g2_ref, b2_ref,
                  wg_ref, wu_ref, wd_ref, g3_ref, b3_ref, out_ref, acc_ref, *, alpha):
    mix = jnp.dot(ya_ref[...], wo_ref[0:D_GROUP, :], preferred_element_type=F32)
    mix += jnp.dot(yl_ref[...], wo_ref[D_GROUP:, :], preferred_element_type=F32)
    x2 = _layer_norm(alpha * x1_ref[...] + mix, g2_ref[...], b2_ref[...])
    ffn = _swiglu(x2.astype(BF16), wg_ref, wu_ref, wd_ref, acc_ref)
    out_ref[...] = _layer_norm(alpha * x2 + 0.5 * ffn, g3_ref[...], b3_ref[...])


def _mix_ffn(x1, ya, yl, w_out, g2, b2, wg, wu, wd, g3, b3, *, alpha, tm):
    n, d = x1.shape
    d_ff = wg.shape[1]
    row = lambda i: (i, 0)
    return pl.pallas_call(
        functools.partial(_mix_ffn_body, alpha=alpha),
        grid=(n // tm,),
        in_specs=[
            pl.BlockSpec((tm, d), row),
            pl.BlockSpec((tm, D_GROUP), row),
            pl.BlockSpec((tm, 3 * D_GROUP), row),
            _resident(w_out.shape), _resident((1, d)), _resident((1, d)),
            _resident((d, d_ff)), _resident((d, d_ff)), _resident((d_ff, d)),
            _resident((1, d)), _resident((1, d)),
        ],
        out_specs=pl.BlockSpec((tm, d), row),
        out_shape=jax.ShapeDtypeStruct((n, d), F32),
        scratch_shapes=[pltpu.VMEM((tm, d), F32)],
        compiler_params=pltpu.CompilerParams(
            dimension_semantics=("arbitrary",), vmem_limit_bytes=V7X_VMEM_LIMIT_BYTES),
        name="mix_ffn",
    )(x1, ya, yl, w_out, g2, b2, wg, wu, wd, g3, b3)


def kernel(x, ffn1_w_gate, ffn1_w_up, ffn1_w_down, ln1_g, ln1_b, mix_w_in, pool_w, pool_scale, conv_w, gmlp_ln_g, gmlp_ln_b, gmlp_w_s, gmlp_b_s, mix_w_out, ln2_g, ln2_b, ffn2_w_gate, ffn2_w_up, ffn2_w_down, ln3_g, ln3_b):
    bsz, seq, d = x.shape
    depth = ffn1_w_gate.shape[0]
    alpha = float((2 * depth) ** 0.25)
    tm = 512
    assert d == HEADS * D_GROUP and mix_w_in.shape[2] == N_PROJ_GROUPS * D_GROUP
    assert seq % tm == 0 and tm % CHUNK == 0 and gmlp_w_s.shape[2] == CHUNK

    wc, f1, g2 = (jnp.asarray(tbl).astype(BF16) for tbl in _dft_tables(seq))
    row = lambda v: v.reshape(1, -1)
    eye = jnp.eye(HEADS, dtype=F32)

    h = x.reshape(bsz * seq, d)
    for l in range(depth):
        bf = lambda w: w[l].astype(BF16)
        pool_bd = (eye[:, None, :, None] * pool_w[l][:, :, None, :]).reshape(D_GROUP, D_GROUP)
        ws_cat = jnp.transpose(gmlp_w_s[l], (1, 0, 2)).reshape(CHUNK, HEADS * CHUNK)
        bs_full = jnp.repeat(gmlp_b_s[l].T, HEAD_DIM, axis=1)

        x1, proj = _ffn_in(h, bf(ffn1_w_gate), bf(ffn1_w_up), bf(ffn1_w_down),
                           row(ln1_g[l]), row(ln1_b[l]), bf(mix_w_in), alpha=alpha, tm=tm)
        ya = _fourier(proj, wc, f1, g2, bsz=bsz, seq=seq)
        yl = _local(proj, pool_bd.astype(BF16), row(pool_scale[l]), conv_w[l],
                    row(gmlp_ln_g[l]), row(gmlp_ln_b[l]), ws_cat.astype(BF16), bs_full,
                    seq=seq, tl=tm)
        h = _mix_ffn(x1, ya, yl, bf(mix_w_out), row(ln2_g[l]), row(ln2_b[l]),
                     bf(ffn2_w_gate), bf(ffn2_w_up), bf(ffn2_w_down),
                     row(ln3_g[l]), row(ln3_b[l]), alpha=alpha, tm=tm)
    return h.reshape(bsz, seq, d)
```

```python
import functools

import numpy as np
import jax
import jax.numpy as jnp
from jax import lax
from jax.experimental import pallas as pl
from jax.experimental.pallas import tpu as pltpu

F32 = jnp.float32
BF16 = jnp.bfloat16

D_GROUP = 256
HEADS = 4
HEAD_DIM = 64
N_PROJ_GROUPS = 7
N_LOCAL_GROUPS = 6
POOL_WINDOWS = (2, 4, 8, 16)
CHUNK = 128
LN_EPS = 1e-5
FF_CHUNK = 256
HALO = 8
DFT_N1 = 64
DFT_N2 = 128

V7X_VMEM_LIMIT_BYTES = 56 * 1024 * 1024


def _resident(shape):
    return pl.BlockSpec(shape, lambda *_: (0,) * len(shape), pipeline_mode=pl.Buffered(1))


def _layer_of(stacked, l):
    blk = (None,) + tuple(stacked.shape[1:])
    idx = (l,) + (0,) * (stacked.ndim - 1)
    return pl.BlockSpec(blk, lambda *_: idx, pipeline_mode=pl.Buffered(1))


def _layer_norm(y, g, b):
    mu = jnp.mean(y, axis=-1, keepdims=True)
    yc = y - mu
    var = jnp.mean(yc * yc, axis=-1, keepdims=True)
    return yc * lax.rsqrt(var + LN_EPS) * g + b


def _swiglu(xb, wg_ref, wu_ref, wd_ref, acc_ref):
    d_ff = wg_ref.shape[1]
    for c in range(d_ff // FF_CHUNK):
        sl = slice(c * FF_CHUNK, (c + 1) * FF_CHUNK)
        g = jnp.dot(xb, wg_ref[:, sl], preferred_element_type=F32)
        u = jnp.dot(xb, wu_ref[:, sl], preferred_element_type=F32)
        h = (g * jax.nn.sigmoid(g) * u).astype(BF16)
        d = jnp.dot(h, wd_ref[sl, :], preferred_element_type=F32)
        if c == 0:
            acc_ref[...] = d
        else:
            acc_ref[...] += d
    return acc_ref[...]


def _ffn_in_body(x_ref, wg_ref, wu_ref, wd_ref, g_ref, b_ref, win_ref,
                 x1_ref, a_ref, loc_ref, acc_ref, *, alpha):
    x = x_ref[...]
    ffn = _swiglu(x.astype(BF16), wg_ref, wu_ref, wd_ref, acc_ref)
    x1 = _layer_norm(alpha * x + 0.5 * ffn, g_ref[...], b_ref[...])
    x1_ref[...] = x1
    x1b = x1.astype(BF16)
    a_ref[...] = jnp.dot(x1b, win_ref[:, 0:D_GROUP], preferred_element_type=F32)
    for j in range(N_LOCAL_GROUPS):
        sl = slice((j + 1) * D_GROUP, (j + 2) * D_GROUP)
        loc_ref[j] = jnp.dot(x1b, win_ref[:, sl], preferred_element_type=F32)


def _ffn_in(x, wg, wu, wd, g, b, w_in, l, *, alpha, tm):
    n, d = x.shape
    row = lambda i: (i, 0)
    return pl.pallas_call(
        functools.partial(_ffn_in_body, alpha=alpha),
        grid=(n // tm,),
        in_specs=[
            pl.BlockSpec((tm, d), row),
            _layer_of(wg, l), _layer_of(wu, l), _layer_of(wd, l),
            _layer_of(g, l), _layer_of(b, l), _layer_of(w_in, l),
        ],
        out_specs=[
            pl.BlockSpec((tm, d), row),
            pl.BlockSpec((tm, D_GROUP), row),
            pl.BlockSpec((N_LOCAL_GROUPS, tm, D_GROUP), lambda i: (0, i, 0)),
        ],
        out_shape=[
            jax.ShapeDtypeStruct((n, d), F32),
            jax.ShapeDtypeStruct((n, D_GROUP), F32),
            jax.ShapeDtypeStruct((N_LOCAL_GROUPS, n, D_GROUP), F32),
        ],
        scratch_shapes=[pltpu.VMEM((tm, d), F32)],
        compiler_params=pltpu.CompilerParams(
            dimension_semantics=("arbitrary",), vmem_limit_bytes=V7X_VMEM_LIMIT_BYTES),
        name="ffn_in",
    )(x, wg, wu, wd, g, b, w_in)


def _dft_tables(seq):
    n1, n2 = DFT_N1, DFT_N2
    assert n1 * n2 == seq
    c = np.arange(HEAD_DIM)
    ang = 2.0 * np.pi * ((c[:, None] * c[None, :]) % HEAD_DIM) / HEAD_DIM
    eye = np.eye(HEADS)
    wc = np.concatenate([np.kron(eye, np.cos(ang)), -np.kron(eye, np.sin(ang))], axis=1)
    k = np.arange(n1)
    ang1 = 2.0 * np.pi * ((k[:, None] * k[None, :]) % n1) / n1
    fr, fi = np.cos(ang1), -np.sin(ang1)
    f1 = np.block([[fr, -fi], [fi, fr]])
    k1 = np.arange(n1)[:, None, None]
    k2 = np.arange(n2)[None, :, None]
    m = np.arange(n2)[None, None, :]
    ang2 = 2.0 * np.pi * ((m * (k1 + n1 * k2)) % seq) / seq
    scale = 1.0 / np.sqrt(float(seq) * HEAD_DIM)
    g2 = np.concatenate([np.cos(ang2), np.sin(ang2)], axis=2) * scale
    return wc.astype(np.float32), f1.astype(np.float32), g2.astype(np.float32)


def _fourier_a_body(a_ref, wc_ref, f1_ref, z_ref, u_ref, *, t):
    n1 = DFT_N1
    for j in range(t):
        sl = slice(j * D_GROUP, (j + 1) * D_GROUP)
        uc = jnp.dot(a_ref[:, sl].astype(BF16), wc_ref[...], preferred_element_type=F32)
        u_ref[0:n1, sl] = uc[:, :D_GROUP].astype(BF16)
        u_ref[n1:2 * n1, sl] = uc[:, D_GROUP:].astype(BF16)
    z = jnp.dot(f1_ref[...], u_ref[...], preferred_element_type=F32)
    z_ref[0] = z[:n1].astype(BF16)
    z_ref[1] = z[n1:].astype(BF16)


def _fourier_b_body(z_ref, g2_ref, y_ref, *, kt):
    for k in range(kt):
        zs = jnp.concatenate([z_ref[0, k], z_ref[1, k]], axis=0)
        r = jnp.dot(g2_ref[k], zs, preferred_element_type=F32)
        y_ref[:, k * D_GROUP:(k + 1) * D_GROUP] = r.astype(BF16)


def _fourier(a, wc, f1, g2, *, bsz, seq, t=8, kt=8):
    n1, n2 = DFT_N1, DFT_N2
    a = a.reshape(bsz, n1, n2 * D_GROUP)
    z = pl.pallas_call(
        functools.partial(_fourier_a_body, t=t),
        grid=(bsz, n2 // t),
        in_specs=[
            pl.BlockSpec((None, n1, t * D_GROUP), lambda b, i: (b, 0, i)),
            _resident(wc.shape), _resident(f1.shape),
        ],
        out_specs=pl.BlockSpec((None, 2, n1, t * D_GROUP), lambda b, i: (b, 0, 0, i)),
        out_shape=jax.ShapeDtypeStruct((bsz, 2, n1, n2 * D_GROUP), BF16),
        scratch_shapes=[pltpu.VMEM((2 * n1, t * D_GROUP), BF16)],
        compiler_params=pltpu.CompilerParams(
            dimension_semantics=("arbitrary", "arbitrary"),
            vmem_limit_bytes=V7X_VMEM_LIMIT_BYTES),
        name="fourier_a",
    )(a, wc, f1)
    z = z.reshape(bsz, 2, n1, n2, D_GROUP)
    y = pl.pallas_call(
        functools.partial(_fourier_b_body, kt=kt),
        grid=(bsz, n1 // kt),
        in_specs=[
            pl.BlockSpec((None, 2, kt, n2, D_GROUP), lambda b, i: (b, 0, i, 0, 0)),
            pl.BlockSpec((kt, n2, 2 * n2), lambda b, i: (i, 0, 0)),
        ],
        out_specs=pl.BlockSpec((None, n2, kt * D_GROUP), lambda b, i: (b, 0, i)),
        out_shape=jax.ShapeDtypeStruct((bsz, n2, n1 * D_GROUP), BF16),
        compiler_params=pltpu.CompilerParams(
            dimension_semantics=("arbitrary", "arbitrary"),
            vmem_limit_bytes=V7X_VMEM_LIMIT_BYTES),
        name="fourier_b",
    )(z, g2)
    return y.reshape(bsz * seq, D_GROUP)


def _shift_rows(x, d):
    return pltpu.roll(x, d % x.shape[0], axis=0)


def _local_body(p_ref, gb_ref, gc_ref, h_ref, u_ref, v_ref,
                pp_ref, pn_ref, gcp_ref, gcn_ref, hp_ref, hn_ref,
                poolw_ref, pscale_ref, convw_ref, lng_ref, lnb_ref, ws_ref, bs_ref,
                y_ref, *, tl, tiles_per_seq, seq):
    ti = lax.rem(pl.program_id(0), tiles_per_seq)
    keep_prev = (ti > 0).astype(F32)
    keep_next = (ti < tiles_per_seq - 1).astype(F32)
    core = slice(HALO, HALO + tl)
    lane = lax.broadcasted_iota(jnp.int32, (1, D_GROUP), 1)
    head = lane // HEAD_DIM

    p = p_ref[...]
    x = jnp.concatenate([pp_ref[...] * keep_prev, p, pn_ref[...] * keep_next], axis=0)
    w2 = x + _shift_rows(x, 1)
    w4 = _shift_rows(w2, 1) + _shift_rows(w2, -1)
    w8 = _shift_rows(w4, 2) + _shift_rows(w4, -2)
    w16 = _shift_rows(w8, 4) + _shift_rows(w8, -4)
    win = jnp.where(head == 0, w2[core],
                    jnp.where(head == 1, w4[core],
                              jnp.where(head == 2, w8[core], w16[core])))
    half = jnp.where(head == 0, POOL_WINDOWS[0] // 2,
                     jnp.where(head == 1, POOL_WINDOWS[1] // 2,
                               jnp.where(head == 2, POOL_WINDOWS[2] // 2,
                                         POOL_WINDOWS[3] // 2)))
    t = ti * tl + lax.broadcasted_iota(jnp.int32, (tl, D_GROUP), 0)
    cnt = jnp.minimum(t + half, seq) - jnp.maximum(t - half, 0)
    pooled = win / cnt.astype(F32) - p
    mixed = jnp.dot(pooled.astype(BF16), poolw_ref[...], preferred_element_type=F32)
    y_ref[:, 0:D_GROUP] = (mixed * pscale_ref[...]).astype(BF16)

    z = jnp.concatenate([gcp_ref[...] * hp_ref[...] * keep_prev,
                         gc_ref[...] * h_ref[...],
                         gcn_ref[...] * hn_ref[...] * keep_next], axis=0)
    conv = (convw_ref[0:1, :] * _shift_rows(z, 1) + convw_ref[1:2, :] * z
            + convw_ref[2:3, :] * _shift_rows(z, -1))
    y_ref[:, D_GROUP:2 * D_GROUP] = (gb_ref[...] * conv[core]).astype(BF16)

    vn = _layer_norm(v_ref[...], lng_ref[...], lnb_ref[...])
    for c in range(tl // CHUNK):
        rows = slice(c * CHUNK, (c + 1) * CHUNK)
        vc = vn[rows]
        stacked = jnp.concatenate(
            [jnp.where(head == hd, vc, 0.0).astype(BF16) for hd in range(HEADS)], axis=0)
        mix = jnp.dot(ws_ref[...], stacked, preferred_element_type=F32) + bs_ref[...]
        y_ref[rows, 2 * D_GROUP:3 * D_GROUP] = (u_ref[rows, :] * mix).astype(BF16)


def _local(loc, pool_bd, pool_scale, conv_w, ln_g, ln_b, ws_cat, bs_full, l, *, seq, tl):
    n = loc.shape[1]
    tiles_per_seq = seq // tl
    hb = tl // HALO
    last_hb = n // HALO - 1

    def main(g):
        return pl.BlockSpec((None, tl, D_GROUP), lambda i: (g, i, 0))

    def prev(g):
        return pl.BlockSpec((None, HALO, D_GROUP),
                            lambda i: (g, jnp.maximum(i * hb - 1, 0), 0))

    def nxt(g):
        return pl.BlockSpec((None, HALO, D_GROUP),
                            lambda i: (g, jnp.minimum((i + 1) * hb, last_hb), 0))

    params = (pool_bd, pool_scale, conv_w, ln_g, ln_b, ws_cat, bs_full)
    return pl.pallas_call(
        functools.partial(_local_body, tl=tl, tiles_per_seq=tiles_per_seq, seq=seq),
        grid=(n // tl,),
        in_specs=[main(0), main(1), main(2), main(3), main(4), main(5),
                  prev(0), nxt(0), prev(2), nxt(2), prev(3), nxt(3)]
                 + [_layer_of(w, l) for w in params],
        out_specs=pl.BlockSpec((tl, 3 * D_GROUP), lambda i: (i, 0)),
        out_shape=jax.ShapeDtypeStruct((n, 3 * D_GROUP), BF16),
        compiler_params=pltpu.CompilerParams(
            dimension_semantics=("arbitrary",), vmem_limit_bytes=V7X_VMEM_LIMIT_BYTES),
        name="local_mixers",
    )(*([loc] * 12), *params)


def _mix_ffn_body(x1_ref, ya_ref, yl_ref, wo_ref, g2_ref, b2_ref,
                  wg_ref, wu_ref, wd_ref, g3_ref, b3_ref, out_ref, acc_ref, *, alpha):
    mix = jnp.dot(ya_ref[...], wo_ref[0:D_GROUP, :], preferred_element_type=F32)
    mix += jnp.dot(yl_ref[...], wo_ref[D_GROUP:, :], preferred_element_type=F32)
    x2 = _layer_norm(alpha * x1_ref[...] + mix, g2_ref[...], b2_ref[...])
    ffn = _swiglu(x2.astype(BF16), wg_ref, wu_ref, wd_ref, acc_ref)
    out_ref[...] = _layer_norm(alpha * x2 + 0.5 * ffn, g3_ref[...], b3_ref[...])


def _mix_ffn(x1, ya, yl, w_out, g2, b2, wg, wu, wd, g3, b3, l, *, alpha, tm):
    n, d = x1.shape
    row = lambda i: (i, 0)
    params = (w_out, g2, b2, wg, wu, wd, g3, b3)
    return pl.pallas_call(
        functools.partial(_mix_ffn_body, alpha=alpha),
        grid=(n // tm,),
        in_specs=[
            pl.BlockSpec((tm, d), row),
            pl.BlockSpec((tm, D_GROUP), row),
            pl.BlockSpec((tm, 3 * D_GROUP), row),
        ] + [_layer_of(w, l) for w in params],
        out_specs=pl.BlockSpec((tm, d), row),
        out_shape=jax.ShapeDtypeStruct((n, d), F32),
        scratch_shapes=[pltpu.VMEM((tm, d), F32)],
        compiler_params=pltpu.CompilerParams(
            dimension_semantics=("arbitrary",), vmem_limit_bytes=V7X_VMEM_LIMIT_BYTES),
        name="mix_ffn",
    )(x1, ya, yl, *params)


def kernel(x, ffn1_w_gate, ffn1_w_up, ffn1_w_down, ln1_g, ln1_b, mix_w_in, pool_w, pool_scale, conv_w, gmlp_ln_g, gmlp_ln_b, gmlp_w_s, gmlp_b_s, mix_w_out, ln2_g, ln2_b, ffn2_w_gate, ffn2_w_up, ffn2_w_down, ln3_g, ln3_b):
    bsz, seq, d = x.shape
    depth = ffn1_w_gate.shape[0]
    alpha = float((2 * depth) ** 0.25)
    tm = 512
    assert d == HEADS * D_GROUP and mix_w_in.shape[2] == N_PROJ_GROUPS * D_GROUP
    assert seq % tm == 0 and tm % CHUNK == 0 and gmlp_w_s.shape[2] == CHUNK

    wc, f1, g2 = (jnp.asarray(tbl).astype(BF16) for tbl in _dft_tables(seq))
    bf = lambda w: w.astype(BF16)
    row = lambda v: v.reshape(depth, 1, -1)

    w1g, w1u, w1d = bf(ffn1_w_gate), bf(ffn1_w_up), bf(ffn1_w_down)
    w2g, w2u, w2d = bf(ffn2_w_gate), bf(ffn2_w_up), bf(ffn2_w_down)
    w_in, w_out = bf(mix_w_in), bf(mix_w_out)
    eye = jnp.eye(HEADS, dtype=F32)
    pool_bd = bf((eye[None, :, None, :, None] * pool_w[:, :, :, None, :])
                 .reshape(depth, D_GROUP, D_GROUP))
    ws_cat = bf(jnp.transpose(gmlp_w_s, (0, 2, 1, 3)).reshape(depth, CHUNK, HEADS * CHUNK))
    bs_full = jnp.repeat(jnp.transpose(gmlp_b_s, (0, 2, 1)), HEAD_DIM, axis=2)

    h = x.reshape(bsz * seq, d)
    for l in range(depth):
        x1, a, loc = _ffn_in(h, w1g, w1u, w1d, row(ln1_g), row(ln1_b), w_in, l,
                             alpha=alpha, tm=tm)
        ya = _fourier(a, wc, f1, g2, bsz=bsz, seq=seq)
        yl = _local(loc, pool_bd, row(pool_scale), conv_w, row(gmlp_ln_g), row(gmlp_ln_b),
                    ws_cat, bs_full, l, seq=seq, tl=tm)
        h = _mix_ffn(x1, ya, yl, w_out, row(ln2_g), row(ln2_b), w2g, w2u, w2d,
                     row(ln3_g), row(ln3_b), l, alpha=alpha, tm=tm)
    return h.reshape(bsz, seq, d)
```

```python
import functools

import numpy as np
import jax
import jax.numpy as jnp
from jax import lax
from jax.experimental import pallas as pl
from jax.experimental.pallas import tpu as pltpu

F32 = jnp.float32
BF16 = jnp.bfloat16

D_GROUP = 256
HEADS = 4
HEAD_DIM = 64
N_PROJ_GROUPS = 7
N_LOCAL_GROUPS = 6
POOL_WINDOWS = (2, 4, 8, 16)
CHUNK = 128
LN_EPS = 1e-5
FF_CHUNK = 256
HALO = 8
DFT_N1 = 64
DFT_N2 = 128

V7X_VMEM_LIMIT_BYTES = 56 * 1024 * 1024


def _resident(shape):
    return pl.BlockSpec(shape, lambda *_: (0,) * len(shape), pipeline_mode=pl.Buffered(1))


def _layer_of(stacked, l):
    blk = (None,) + tuple(stacked.shape[1:])
    idx = (l,) + (0,) * (stacked.ndim - 1)
    return pl.BlockSpec(blk, lambda *_: idx, pipeline_mode=pl.Buffered(1))


def _layer_norm(y, g, b):
    mu = jnp.mean(y, axis=-1, keepdims=True)
    yc = y - mu
    var = jnp.mean(yc * yc, axis=-1, keepdims=True)
    return yc * lax.rsqrt(var + LN_EPS) * g + b


def _swiglu(xb, wg_ref, wu_ref, wd_ref, acc_ref):
    d_ff = wg_ref.shape[1]
    for c in range(d_ff // FF_CHUNK):
        sl = slice(c * FF_CHUNK, (c + 1) * FF_CHUNK)
        g = jnp.dot(xb, wg_ref[:, sl], preferred_element_type=F32)
        u = jnp.dot(xb, wu_ref[:, sl], preferred_element_type=F32)
        h = (g * jax.nn.sigmoid(g) * u).astype(BF16)
        d = jnp.dot(h, wd_ref[sl, :], preferred_element_type=F32)
        if c == 0:
            acc_ref[...] = d
        else:
            acc_ref[...] += d
    return acc_ref[...]


def _ffn_in_body(x_ref, wg_ref, wu_ref, wd_ref, g_ref, b_ref, win_ref,
                 x1_ref, a_ref, loc_ref, acc_ref, *, alpha):
    x = x_ref[...]
    ffn = _swiglu(x.astype(BF16), wg_ref, wu_ref, wd_ref, acc_ref)
    x1 = _layer_norm(alpha * x + 0.5 * ffn, g_ref[...], b_ref[...])
    x1_ref[...] = x1
    x1b = x1.astype(BF16)
    a_ref[...] = jnp.dot(x1b, win_ref[:, 0:D_GROUP], preferred_element_type=F32)
    for j in range(N_LOCAL_GROUPS):
        sl = slice((j + 1) * D_GROUP, (j + 2) * D_GROUP)
        loc_ref[j] = jnp.dot(x1b, win_ref[:, sl], preferred_element_type=F32)


def _ffn_in(x, wg, wu, wd, g, b, w_in, l, *, alpha, tm):
    n, d = x.shape
    row = lambda i: (i, 0)
    return pl.pallas_call(
        functools.partial(_ffn_in_body, alpha=alpha),
        grid=(n // tm,),
        in_specs=[
            pl.BlockSpec((tm, d), row),
            _layer_of(wg, l), _layer_of(wu, l), _layer_of(wd, l),
            _layer_of(g, l), _layer_of(b, l), _layer_of(w_in, l),
        ],
        out_specs=[
            pl.BlockSpec((tm, d), row),
            pl.BlockSpec((tm, D_GROUP), row),
            pl.BlockSpec((N_LOCAL_GROUPS, tm, D_GROUP), lambda i: (0, i, 0)),
        ],
        out_shape=[
            jax.ShapeDtypeStruct((n, d), F32),
            jax.ShapeDtypeStruct((n, D_GROUP), F32),
            jax.ShapeDtypeStruct((N_LOCAL_GROUPS, n, D_GROUP), F32),
        ],
        scratch_shapes=[pltpu.VMEM((tm, d), F32)],
        compiler_params=pltpu.CompilerParams(
            dimension_semantics=("arbitrary",), vmem_limit_bytes=V7X_VMEM_LIMIT_BYTES),
        name="ffn_in",
    )(x, wg, wu, wd, g, b, w_in)


def _dft_tables(seq):
    n1, n2 = DFT_N1, DFT_N2
    assert n1 * n2 == seq
    c = np.arange(HEAD_DIM)
    ang = 2.0 * np.pi * ((c[:, None] * c[None, :]) % HEAD_DIM) / HEAD_DIM
    eye = np.eye(HEADS)
    wc = np.concatenate([np.kron(eye, np.cos(ang)), -np.kron(eye, np.sin(ang))], axis=1)
    k = np.arange(n1)
    ang1 = 2.0 * np.pi * ((k[:, None] * k[None, :]) % n1) / n1
    fr, fi = np.cos(ang1), -np.sin(ang1)
    f1 = np.block([[fr, -fi], [fi, fr]])
    k1 = np.arange(n1)[:, None, None]
    k2 = np.arange(n2)[None, :, None]
    m = np.arange(n2)[None, None, :]
    ang2 = 2.0 * np.pi * ((m * (k1 + n1 * k2)) % seq) / seq
    scale = 1.0 / np.sqrt(float(seq) * HEAD_DIM)
    g2 = np.concatenate([np.cos(ang2), np.sin(ang2)], axis=2) * scale
    return wc.astype(np.float32), f1.astype(np.float32), g2.astype(np.float32)


def _fourier_a_body(a_ref, wc_ref, f1_ref, z_ref, u_ref, *, t):
    n1 = DFT_N1
    for j in range(t):
        sl = slice(j * D_GROUP, (j + 1) * D_GROUP)
        uc = jnp.dot(a_ref[:, sl].astype(BF16), wc_ref[...], preferred_element_type=F32)
        u_ref[0:n1, sl] = uc[:, :D_GROUP].astype(BF16)
        u_ref[n1:2 * n1, sl] = uc[:, D_GROUP:].astype(BF16)
    z = jnp.dot(f1_ref[...], u_ref[...], preferred_element_type=F32)
    z_ref[0] = z[:n1].astype(BF16)
    z_ref[1] = z[n1:].astype(BF16)


def _fourier_b_body(z_ref, g2_ref, y_ref, *, kt):
    for k in range(kt):
        zs = jnp.concatenate([z_ref[0, k], z_ref[1, k]], axis=0)
        r = jnp.dot(g2_ref[k], zs, preferred_element_type=F32)
        y_ref[:, k * D_GROUP:(k + 1) * D_GROUP] = r.astype(BF16)


def _fourier(a, wc, f1, g2, *, bsz, seq, t=8, kt=8):
    n1, n2 = DFT_N1, DFT_N2
    a = a.reshape(bsz, n1, n2 * D_GROUP)
    z = pl.pallas_call(
        functools.partial(_fourier_a_body, t=t),
        grid=(bsz, n2 // t),
        in_specs=[
            pl.BlockSpec((None, n1, t * D_GROUP), lambda b, i: (b, 0, i)),
            _resident(wc.shape), _resident(f1.shape),
        ],
        out_specs=pl.BlockSpec((None, 2, n1, t * D_GROUP), lambda b, i: (b, 0, 0, i)),
        out_shape=jax.ShapeDtypeStruct((bsz, 2, n1, n2 * D_GROUP), BF16),
        scratch_shapes=[pltpu.VMEM((2 * n1, t * D_GROUP), BF16)],
        compiler_params=pltpu.CompilerParams(
            dimension_semantics=("arbitrary", "arbitrary"),
            vmem_limit_bytes=V7X_VMEM_LIMIT_BYTES),
        name="fourier_a",
    )(a, wc, f1)
    z = z.reshape(bsz, 2, n1, n2, D_GROUP)
    y = pl.pallas_call(
        functools.partial(_fourier_b_body, kt=kt),
        grid=(bsz, n1 // kt),
        in_specs=[
            pl.BlockSpec((None, 2, kt, n2, D_GROUP), lambda b, i: (b, 0, i, 0, 0)),
            pl.BlockSpec((kt, n2, 2 * n2), lambda b, i: (i, 0, 0)),
        ],
        out_specs=pl.BlockSpec((None, n2, kt * D_GROUP), lambda b, i: (b, 0, i)),
        out_shape=jax.ShapeDtypeStruct((bsz, n2, n1 * D_GROUP), BF16),
        compiler_params=pltpu.CompilerParams(
            dimension_semantics=("arbitrary", "arbitrary"),
            vmem_limit_bytes=V7X_VMEM_LIMIT_BYTES),
        name="fourier_b",
    )(z, g2)
    return y.reshape(bsz * seq, D_GROUP)


def _shift_rows(x, d):
    return pltpu.roll(x, d % x.shape[0], axis=0)


def _pool_mixer(p, p_prev, p_next, pool_bd, pool_scale, head, t0, seq):
    tl = p.shape[0]
    core = slice(HALO, HALO + tl)
    x = jnp.concatenate([p_prev, p, p_next], axis=0)
    w2 = x + _shift_rows(x, 1)
    w4 = _shift_rows(w2, 1) + _shift_rows(w2, -1)
    w8 = _shift_rows(w4, 2) + _shift_rows(w4, -2)
    w16 = _shift_rows(w8, 4) + _shift_rows(w8, -4)
    win = jnp.where(head == 0, w2[core],
                    jnp.where(head == 1, w4[core],
                              jnp.where(head == 2, w8[core], w16[core])))
    half = jnp.where(head == 0, POOL_WINDOWS[0] // 2,
                     jnp.where(head == 1, POOL_WINDOWS[1] // 2,
                               jnp.where(head == 2, POOL_WINDOWS[2] // 2,
                                         POOL_WINDOWS[3] // 2)))
    t = t0 + lax.broadcasted_iota(jnp.int32, (tl, D_GROUP), 0)
    cnt = jnp.minimum(t + half, seq) - jnp.maximum(t - half, 0)
    pooled = win / cnt.astype(F32) - p
    mixed = jnp.dot(pooled.astype(BF16), pool_bd, preferred_element_type=F32)
    return (mixed * pool_scale).astype(BF16)


def _conv_mixer(gate_b, z, z_prev, z_next, conv_w):
    tl = z.shape[0]
    zz = jnp.concatenate([z_prev, z, z_next], axis=0)
    conv = (conv_w[0:1, :] * _shift_rows(zz, 1) + conv_w[1:2, :] * zz
            + conv_w[2:3, :] * _shift_rows(zz, -1))
    return (gate_b * conv[HALO:HALO + tl]).astype(BF16)


def _gmlp_mixer(u, v, ln_g, ln_b, ws_cat, bs_full, head):
    vn = _layer_norm(v, ln_g, ln_b)
    out = []
    for c in range(v.shape[0] // CHUNK):
        rows = slice(c * CHUNK, (c + 1) * CHUNK)
        vc = vn[rows]
        stacked = jnp.concatenate(
            [jnp.where(head == hd, vc, 0.0).astype(BF16) for hd in range(HEADS)], axis=0)
        mix = jnp.dot(ws_cat, stacked, preferred_element_type=F32) + bs_full
        out.append((u[rows] * mix).astype(BF16))
    return jnp.concatenate(out, axis=0)


def _mix_ffn_body(x1_ref, ya_ref,
                  p_ref, gb_ref, gc_ref, h_ref, u_ref, v_ref,
                  pp_ref, pn_ref, gcp_ref, gcn_ref, hp_ref, hn_ref,
                  poolw_ref, pscale_ref, convw_ref, lng_ref, lnb_ref, ws_ref, bs_ref,
                  wo_ref, g2_ref, b2_ref, wg_ref, wu_ref, wd_ref, g3_ref, b3_ref,
                  out_ref, acc_ref, *, alpha, tiles_per_seq, seq):
    tm = x1_ref.shape[0]
    ti = lax.rem(pl.program_id(0), tiles_per_seq)
    keep_prev = (ti > 0).astype(F32)
    keep_next = (ti < tiles_per_seq - 1).astype(F32)
    head = lax.broadcasted_iota(jnp.int32, (1, D_GROUP), 1) // HEAD_DIM

    yb = _pool_mixer(p_ref[...], pp_ref[...] * keep_prev, pn_ref[...] * keep_next,
                     poolw_ref[...], pscale_ref[...], head, ti * tm, seq)
    yc = _conv_mixer(gb_ref[...], gc_ref[...] * h_ref[...],
                     gcp_ref[...] * hp_ref[...] * keep_prev,
                     gcn_ref[...] * hn_ref[...] * keep_next, convw_ref[...])
    yd = _gmlp_mixer(u_ref[...], v_ref[...], lng_ref[...], lnb_ref[...],
                     ws_ref[...], bs_ref[...], head)

    mix = jnp.dot(ya_ref[...], wo_ref[0:D_GROUP, :], preferred_element_type=F32)
    for j, y in enumerate((yb, yc, yd)):
        mix += jnp.dot(y, wo_ref[(j + 1) * D_GROUP:(j + 2) * D_GROUP, :],
                       preferred_element_type=F32)
    x2 = _layer_norm(alpha * x1_ref[...] + mix, g2_ref[...], b2_ref[...])
    ffn = _swiglu(x2.astype(BF16), wg_ref, wu_ref, wd_ref, acc_ref)
    out_ref[...] = _layer_norm(alpha * x2 + 0.5 * ffn, g3_ref[...], b3_ref[...])


def _mix_ffn(x1, ya, loc, local_params, w_out, g2, b2, wg, wu, wd, g3, b3, l,
             *, alpha, tm, seq):
    n, d = x1.shape
    tiles_per_seq = seq // tm
    hb = tm // HALO
    last_hb = n // HALO - 1
    row = lambda i: (i, 0)

    def main(g):
        return pl.BlockSpec((None, tm, D_GROUP), lambda i: (g, i, 0))

    def prev(g):
        return pl.BlockSpec((None, HALO, D_GROUP),
                            lambda i: (g, jnp.maximum(i * hb - 1, 0), 0))

    def nxt(g):
        return pl.BlockSpec((None, HALO, D_GROUP),
                            lambda i: (g, jnp.minimum((i + 1) * hb, last_hb), 0))

    params = tuple(local_params) + (w_out, g2, b2, wg, wu, wd, g3, b3)
    return pl.pallas_call(
        functools.partial(_mix_ffn_body, alpha=alpha, tiles_per_seq=tiles_per_seq, seq=seq),
        grid=(n // tm,),
        in_specs=[pl.BlockSpec((tm, d), row), pl.BlockSpec((tm, D_GROUP), row),
                  main(0), main(1), main(2), main(3), main(4), main(5),
                  prev(0), nxt(0), prev(2), nxt(2), prev(3), nxt(3)]
                 + [_layer_of(w, l) for w in params],
        out_specs=pl.BlockSpec((tm, d), row),
        out_shape=jax.ShapeDtypeStruct((n, d), F32),
        scratch_shapes=[pltpu.VMEM((tm, d), F32)],
        compiler_params=pltpu.CompilerParams(
            dimension_semantics=("arbitrary",), vmem_limit_bytes=V7X_VMEM_LIMIT_BYTES),
        name="mix_ffn",
    )(x1, ya, *([loc] * 12), *params)


def kernel(x, ffn1_w_gate, ffn1_w_up, ffn1_w_down, ln1_g, ln1_b, mix_w_in, pool_w, pool_scale, conv_w, gmlp_ln_g, gmlp_ln_b, gmlp_w_s, gmlp_b_s, mix_w_out, ln2_g, ln2_b, ffn2_w_gate, ffn2_w_up, ffn2_w_down, ln3_g, ln3_b):
    bsz, seq, d = x.shape
    depth = ffn1_w_gate.shape[0]
    alpha = float((2 * depth) ** 0.25)
    tm = 512
    assert d == HEADS * D_GROUP and mix_w_in.shape[2] == N_PROJ_GROUPS * D_GROUP
    assert seq % tm == 0 and tm % CHUNK == 0 and gmlp_w_s.shape[2] == CHUNK

    wc, f1, g2 = (jnp.asarray(tbl).astype(BF16) for tbl in _dft_tables(seq))
    bf = lambda w: w.astype(BF16)
    row = lambda v: v.reshape(depth, 1, -1)

    w1g, w1u, w1d = bf(ffn1_w_gate), bf(ffn1_w_up), bf(ffn1_w_down)
    w2g, w2u, w2d = bf(ffn2_w_gate), bf(ffn2_w_up), bf(ffn2_w_down)
    w_in, w_out = bf(mix_w_in), bf(mix_w_out)
    eye = jnp.eye(HEADS, dtype=F32)
    pool_bd = bf((eye[None, :, None, :, None] * pool_w[:, :, :, None, :])
                 .reshape(depth, D_GROUP, D_GROUP))
    ws_cat = bf(jnp.transpose(gmlp_w_s, (0, 2, 1, 3)).reshape(depth, CHUNK, HEADS * CHUNK))
    bs_full = jnp.repeat(jnp.transpose(gmlp_b_s, (0, 2, 1)), HEAD_DIM, axis=2)
    local_params = (pool_bd, row(pool_scale), conv_w, row(gmlp_ln_g), row(gmlp_ln_b),
                    ws_cat, bs_full)

    h = x.reshape(bsz * seq, d)
    for l in range(depth):
        x1, a, loc = _ffn_in(h, w1g, w1u, w1d, row(ln1_g), row(ln1_b), w_in, l,
                             alpha=alpha, tm=tm)
        ya = _fourier(a, wc, f1, g2, bsz=bsz, seq=seq)
        h = _mix_ffn(x1, ya, loc, local_params, w_out, row(ln2_g), row(ln2_b),
                     w2g, w2u, w2d, row(ln3_g), row(ln3_b), l, alpha=alpha, tm=tm, seq=seq)
    return h.reshape(bsz, seq, d)
```

```python
import functools

import numpy as np
import jax
import jax.numpy as jnp
from jax import lax
from jax.experimental import pallas as pl
from jax.experimental.pallas import tpu as pltpu

F32 = jnp.float32
BF16 = jnp.bfloat16

D_GROUP = 256
HEADS = 4
HEAD_DIM = 64
N_PROJ_GROUPS = 7
N_LOCAL_GROUPS = 6
POOL_WINDOWS = (2, 4, 8, 16)
CHUNK = 128
LN_EPS = 1e-5
FF_CHUNK = 256
HALO = 8
BF16_SUBLANES = 16
DFT_N1 = 64
DFT_N2 = 128

V7X_VMEM_LIMIT_BYTES = 56 * 1024 * 1024


def _resident(shape):
    return pl.BlockSpec(shape, lambda *_: (0,) * len(shape), pipeline_mode=pl.Buffered(1))


def _layer_of(stacked, l):
    blk = (None,) + tuple(stacked.shape[1:])
    idx = (l,) + (0,) * (stacked.ndim - 1)
    return pl.BlockSpec(blk, lambda *_: idx, pipeline_mode=pl.Buffered(1))


def _cast_plan(w, l, n_steps):
    r, c = w.shape[1:]
    steps_per_chunk = 1
    while (r * steps_per_chunk) % n_steps or (r * steps_per_chunk // n_steps) % BF16_SUBLANES:
        steps_per_chunk *= 2
        assert steps_per_chunk <= n_steps, (w.shape, n_steps)
    rows = r * steps_per_chunk // n_steps
    in_spec = pl.BlockSpec((None, rows, c), lambda i: (l, i // steps_per_chunk, 0))
    out_spec = pl.BlockSpec((rows, c), lambda i: (i // steps_per_chunk, 0))
    return in_spec, out_spec, jax.ShapeDtypeStruct((r, c), BF16)


def _cast_chunks(src_refs, dst_refs):
    for src, dst in zip(src_refs, dst_refs):
        dst[...] = src[...].astype(BF16)


def _layer_norm(y, g, b):
    mu = jnp.mean(y, axis=-1, keepdims=True)
    yc = y - mu
    var = jnp.mean(yc * yc, axis=-1, keepdims=True)
    return yc * lax.rsqrt(var + LN_EPS) * g + b


def _swiglu(xb, wg_ref, wu_ref, wd_ref, acc_ref):
    d_ff = wg_ref.shape[1]
    for c in range(d_ff // FF_CHUNK):
        sl = slice(c * FF_CHUNK, (c + 1) * FF_CHUNK)
        g = jnp.dot(xb, wg_ref[:, sl], preferred_element_type=F32)
        u = jnp.dot(xb, wu_ref[:, sl], preferred_element_type=F32)
        h = (g * jax.nn.sigmoid(g) * u).astype(BF16)
        d = jnp.dot(h, wd_ref[sl, :], preferred_element_type=F32)
        if c == 0:
            acc_ref[...] = d
        else:
            acc_ref[...] += d
    return acc_ref[...]


def _ffn_in_body(*refs, alpha, n_cast):
    x_ref, wg_ref, wu_ref, wd_ref, g_ref, b_ref, win_ref = refs[:7]
    cast_src = refs[7:7 + n_cast]
    x1_ref, a_ref, loc_ref = refs[7 + n_cast:10 + n_cast]
    cast_dst = refs[10 + n_cast:10 + 2 * n_cast]
    acc_ref = refs[10 + 2 * n_cast]

    x = x_ref[...]
    ffn = _swiglu(x.astype(BF16), wg_ref, wu_ref, wd_ref, acc_ref)
    x1 = _layer_norm(alpha * x + 0.5 * ffn, g_ref[...], b_ref[...])
    x1_ref[...] = x1
    x1b = x1.astype(BF16)
    a_ref[...] = jnp.dot(x1b, win_ref[:, 0:D_GROUP], preferred_element_type=F32)
    for j in range(N_LOCAL_GROUPS):
        sl = slice((j + 1) * D_GROUP, (j + 2) * D_GROUP)
        loc_ref[j] = jnp.dot(x1b, win_ref[:, sl], preferred_element_type=F32)
    _cast_chunks(cast_src, cast_dst)


def _ffn_in(x, wg, wu, wd, g, b, w_in, next_weights, l, *, alpha, tm):
    n, d = x.shape
    n_steps = n // tm
    row = lambda i: (i, 0)
    plans = [_cast_plan(w, l, n_steps) for w in next_weights]
    out = pl.pallas_call(
        functools.partial(_ffn_in_body, alpha=alpha, n_cast=len(plans)),
        grid=(n_steps,),
        in_specs=[
            pl.BlockSpec((tm, d), row),
            _resident(wg.shape), _resident(wu.shape), _resident(wd.shape),
            _layer_of(g, l), _layer_of(b, l), _resident(w_in.shape),
        ] + [p[0] for p in plans],
        out_specs=[
            pl.BlockSpec((tm, d), row),
            pl.BlockSpec((tm, D_GROUP), row),
            pl.BlockSpec((N_LOCAL_GROUPS, tm, D_GROUP), lambda i: (0, i, 0)),
        ] + [p[1] for p in plans],
        out_shape=[
            jax.ShapeDtypeStruct((n, d), F32),
            jax.ShapeDtypeStruct((n, D_GROUP), F32),
            jax.ShapeDtypeStruct((N_LOCAL_GROUPS, n, D_GROUP), F32),
        ] + [p[2] for p in plans],
        scratch_shapes=[pltpu.VMEM((tm, d), F32)],
        compiler_params=pltpu.CompilerParams(
            dimension_semantics=("arbitrary",), vmem_limit_bytes=V7X_VMEM_LIMIT_BYTES),
        name="ffn_in",
    )(x, wg, wu, wd, g, b, w_in, *next_weights)
    return out[0], out[1], out[2], tuple(out[3:])


def _dft_tables(seq):
    n1, n2 = DFT_N1, DFT_N2
    assert n1 * n2 == seq
    c = np.arange(HEAD_DIM)
    ang = 2.0 * np.pi * ((c[:, None] * c[None, :]) % HEAD_DIM) / HEAD_DIM
    eye = np.eye(HEADS)
    wc = np.concatenate([np.kron(eye, np.cos(ang)), -np.kron(eye, np.sin(ang))], axis=1)
    k = np.arange(n1)
    ang1 = 2.0 * np.pi * ((k[:, None] * k[None, :]) % n1) / n1
    fr, fi = np.cos(ang1), -np.sin(ang1)
    f1 = np.block([[fr, -fi], [fi, fr]])
    k1 = np.arange(n1)[:, None, None]
    k2 = np.arange(n2)[None, :, None]
    m = np.arange(n2)[None, None, :]
    ang2 = 2.0 * np.pi * ((m * (k1 + n1 * k2)) % seq) / seq
    scale = 1.0 / np.sqrt(float(seq) * HEAD_DIM)
    g2 = np.concatenate([np.cos(ang2), np.sin(ang2)], axis=2) * scale
    return wc.astype(np.float32), f1.astype(np.float32), g2.astype(np.float32)


def _fourier_a_body(a_ref, wc_ref, f1_ref, z_ref, u_ref, *, t):
    n1 = DFT_N1
    for j in range(t):
        sl = slice(j * D_GROUP, (j + 1) * D_GROUP)
        uc = jnp.dot(a_ref[:, sl].astype(BF16), wc_ref[...], preferred_element_type=F32)
        u_ref[0:n1, sl] = uc[:, :D_GROUP].astype(BF16)
        u_ref[n1:2 * n1, sl] = uc[:, D_GROUP:].astype(BF16)
    z = jnp.dot(f1_ref[...], u_ref[...], preferred_element_type=F32)
    z_ref[0] = z[:n1].astype(BF16)
    z_ref[1] = z[n1:].astype(BF16)


def _fourier_b_body(z_ref, g2_ref, y_ref, *, kt):
    for k in range(kt):
        zs = jnp.concatenate([z_ref[0, k], z_ref[1, k]], axis=0)
        r = jnp.dot(g2_ref[k], zs, preferred_element_type=F32)
        y_ref[:, k * D_GROUP:(k + 1) * D_GROUP] = r.astype(BF16)


def _fourier(a, wc, f1, g2, *, bsz, seq, t=8, kt=8):
    n1, n2 = DFT_N1, DFT_N2
    a = a.reshape(bsz, n1, n2 * D_GROUP)
    z = pl.pallas_call(
        functools.partial(_fourier_a_body, t=t),
        grid=(bsz, n2 // t),
        in_specs=[
            pl.BlockSpec((None, n1, t * D_GROUP), lambda b, i: (b, 0, i)),
            _resident(wc.shape), _resident(f1.shape),
        ],
        out_specs=pl.BlockSpec((None, 2, n1, t * D_GROUP), lambda b, i: (b, 0, 0, i)),
        out_shape=jax.ShapeDtypeStruct((bsz, 2, n1, n2 * D_GROUP), BF16),
        scratch_shapes=[pltpu.VMEM((2 * n1, t * D_GROUP), BF16)],
        compiler_params=pltpu.CompilerParams(
            dimension_semantics=("arbitrary", "arbitrary"),
            vmem_limit_bytes=V7X_VMEM_LIMIT_BYTES),
        name="fourier_a",
    )(a, wc, f1)
    z = z.reshape(bsz, 2, n1, n2, D_GROUP)
    y = pl.pallas_call(
        functools.partial(_fourier_b_body, kt=kt),
        grid=(bsz, n1 // kt),
        in_specs=[
            pl.BlockSpec((None, 2, kt, n2, D_GROUP), lambda b, i: (b, 0, i, 0, 0)),
            pl.BlockSpec((kt, n2, 2 * n2), lambda b, i: (i, 0, 0)),
        ],
        out_specs=pl.BlockSpec((None, n2, kt * D_GROUP), lambda b, i: (b, 0, i)),
        out_shape=jax.ShapeDtypeStruct((bsz, n2, n1 * D_GROUP), BF16),
        compiler_params=pltpu.CompilerParams(
            dimension_semantics=("arbitrary", "arbitrary"),
            vmem_limit_bytes=V7X_VMEM_LIMIT_BYTES),
        name="fourier_b",
    )(z, g2)
    return y.reshape(bsz * seq, D_GROUP)


def _shift_rows(x, d):
    return pltpu.roll(x, d % x.shape[0], axis=0)


def _pool_mixer(p, p_prev, p_next, pool_bd, pool_scale, head, t0, seq):
    tl = p.shape[0]
    core = slice(HALO, HALO + tl)
    x = jnp.concatenate([p_prev, p, p_next], axis=0)
    w2 = x + _shift_rows(x, 1)
    w4 = _shift_rows(w2, 1) + _shift_rows(w2, -1)
    w8 = _shift_rows(w4, 2) + _shift_rows(w4, -2)
    w16 = _shift_rows(w8, 4) + _shift_rows(w8, -4)
    win = jnp.where(head == 0, w2[core],
                    jnp.where(head == 1, w4[core],
                              jnp.where(head == 2, w8[core], w16[core])))
    half = jnp.where(head == 0, POOL_WINDOWS[0] // 2,
                     jnp.where(head == 1, POOL_WINDOWS[1] // 2,
                               jnp.where(head == 2, POOL_WINDOWS[2] // 2,
                                         POOL_WINDOWS[3] // 2)))
    t = t0 + lax.broadcasted_iota(jnp.int32, (tl, D_GROUP), 0)
    cnt = jnp.minimum(t + half, seq) - jnp.maximum(t - half, 0)
    pooled = win / cnt.astype(F32) - p
    mixed = jnp.dot(pooled.astype(BF16), pool_bd, preferred_element_type=F32)
    return (mixed * pool_scale).astype(BF16)


def _conv_mixer(gate_b, z, z_prev, z_next, conv_w):
    tl = z.shape[0]
    zz = jnp.concatenate([z_prev, z, z_next], axis=0)
    conv = (conv_w[0:1, :] * _shift_rows(zz, 1) + conv_w[1:2, :] * zz
            + conv_w[2:3, :] * _shift_rows(zz, -1))
    return (gate_b * conv[HALO:HALO + tl]).astype(BF16)


def _gmlp_mixer(u, v, ln_g, ln_b, ws_ref, bs_full, head):
    vn = _layer_norm(v, ln_g, ln_b)
    ws_cat = jnp.concatenate([ws_ref[hd] for hd in range(HEADS)], axis=1)
    out = []
    for c in range(v.shape[0] // CHUNK):
        rows = slice(c * CHUNK, (c + 1) * CHUNK)
        vc = vn[rows]
        stacked = jnp.concatenate(
            [jnp.where(head == hd, vc, 0.0).astype(BF16) for hd in range(HEADS)], axis=0)
        mix = jnp.dot(ws_cat, stacked, preferred_element_type=F32) + bs_full
        out.append((u[rows] * mix).astype(BF16))
    return jnp.concatenate(out, axis=0)


N_MIX_FIXED_INPUTS = 29


def _mix_ffn_body(*refs, alpha, tiles_per_seq, seq, n_cast):
    (x1_ref, ya_ref,
     p_ref, gb_ref, gc_ref, h_ref, u_ref, v_ref,
     pp_ref, pn_ref, gcp_ref, gcn_ref, hp_ref, hn_ref,
     poolw_ref, pscale_ref, convw_ref, lng_ref, lnb_ref, ws_ref, bs_ref,
     wo_ref, g2_ref, b2_ref, wg_ref, wu_ref, wd_ref, g3_ref, b3_ref) = refs[:N_MIX_FIXED_INPUTS]
    cast_src = refs[N_MIX_FIXED_INPUTS:N_MIX_FIXED_INPUTS + n_cast]
    out_ref = refs[N_MIX_FIXED_INPUTS + n_cast]
    cast_dst = refs[N_MIX_FIXED_INPUTS + n_cast + 1:N_MIX_FIXED_INPUTS + 2 * n_cast + 1]
    acc_ref = refs[N_MIX_FIXED_INPUTS + 2 * n_cast + 1]

    tm = x1_ref.shape[0]
    ti = lax.rem(pl.program_id(0), tiles_per_seq)
    keep_prev = (ti > 0).astype(F32)
    keep_next = (ti < tiles_per_seq - 1).astype(F32)
    head = lax.broadcasted_iota(jnp.int32, (1, D_GROUP), 1) // HEAD_DIM

    yb = _pool_mixer(p_ref[...], pp_ref[...] * keep_prev, pn_ref[...] * keep_next,
                     poolw_ref[...], pscale_ref[...], head, ti * tm, seq)
    yc = _conv_mixer(gb_ref[...], gc_ref[...] * h_ref[...],
                     gcp_ref[...] * hp_ref[...] * keep_prev,
                     gcn_ref[...] * hn_ref[...] * keep_next, convw_ref[...])
    yd = _gmlp_mixer(u_ref[...], v_ref[...], lng_ref[...], lnb_ref[...],
                     ws_ref, bs_ref[...], head)

    mix = jnp.dot(ya_ref[...], wo_ref[0:D_GROUP, :], preferred_element_type=F32)
    for j, y in enumerate((yb, yc, yd)):
        mix += jnp.dot(y, wo_ref[(j + 1) * D_GROUP:(j + 2) * D_GROUP, :],
                       preferred_element_type=F32)
    x2 = _layer_norm(alpha * x1_ref[...] + mix, g2_ref[...], b2_ref[...])
    ffn = _swiglu(x2.astype(BF16), wg_ref, wu_ref, wd_ref, acc_ref)
    out_ref[...] = _layer_norm(alpha * x2 + 0.5 * ffn, g3_ref[...], b3_ref[...])
    _cast_chunks(cast_src, cast_dst)


def _mix_ffn(x1, ya, loc, local_params, w_out, g2, b2, wg, wu, wd, g3, b3, next_weights, l,
             *, alpha, tm, seq):
    n, d = x1.shape
    n_steps = n // tm
    tiles_per_seq = seq // tm
    hb = tm // HALO
    last_hb = n // HALO - 1
    row = lambda i: (i, 0)

    def main(g):
        return pl.BlockSpec((None, tm, D_GROUP), lambda i: (g, i, 0))

    def prev(g):
        return pl.BlockSpec((None, HALO, D_GROUP),
                            lambda i: (g, jnp.maximum(i * hb - 1, 0), 0))

    def nxt(g):
        return pl.BlockSpec((None, HALO, D_GROUP),
                            lambda i: (g, jnp.minimum((i + 1) * hb, last_hb), 0))

    plans = [_cast_plan(w, l + 1, n_steps) for w in next_weights]
    in_specs = ([pl.BlockSpec((tm, d), row), pl.BlockSpec((tm, D_GROUP), row),
                 main(0), main(1), main(2), main(3), main(4), main(5),
                 prev(0), nxt(0), prev(2), nxt(2), prev(3), nxt(3)]
                + [_layer_of(w, l) for w in local_params]
                + [_resident(w_out.shape), _layer_of(g2, l), _layer_of(b2, l),
                   _resident(wg.shape), _resident(wu.shape), _resident(wd.shape),
                   _layer_of(g3, l), _layer_of(b3, l)])
    assert len(in_specs) == N_MIX_FIXED_INPUTS
    out = pl.pallas_call(
        functools.partial(_mix_ffn_body, alpha=alpha, tiles_per_seq=tiles_per_seq, seq=seq,
                          n_cast=len(plans)),
        grid=(n_steps,),
        in_specs=in_specs + [p[0] for p in plans],
        out_specs=[pl.BlockSpec((tm, d), row)] + [p[1] for p in plans],
        out_shape=[jax.ShapeDtypeStruct((n, d), F32)] + [p[2] for p in plans],
        scratch_shapes=[pltpu.VMEM((tm, d), F32)],
        compiler_params=pltpu.CompilerParams(
            dimension_semantics=("arbitrary",), vmem_limit_bytes=V7X_VMEM_LIMIT_BYTES),
        name="mix_ffn",
    )(x1, ya, *([loc] * 12), *local_params, w_out, g2, b2, wg, wu, wd, g3, b3, *next_weights)
    return out[0], tuple(out[1:])


def kernel(x, ffn1_w_gate, ffn1_w_up, ffn1_w_down, ln1_g, ln1_b, mix_w_in, pool_w, pool_scale, conv_w, gmlp_ln_g, gmlp_ln_b, gmlp_w_s, gmlp_b_s, mix_w_out, ln2_g, ln2_b, ffn2_w_gate, ffn2_w_up, ffn2_w_down, ln3_g, ln3_b):
    bsz, seq, d = x.shape
    depth = ffn1_w_gate.shape[0]
    alpha = float((2 * depth) ** 0.25)
    tm = 512
    assert d == HEADS * D_GROUP and mix_w_in.shape[2] == N_PROJ_GROUPS * D_GROUP
    assert seq % tm == 0 and tm % CHUNK == 0 and gmlp_w_s.shape[2] == CHUNK

    wc, f1, g2 = (jnp.asarray(tbl).astype(BF16) for tbl in _dft_tables(seq))
    bf = lambda w: w.astype(BF16)
    row = lambda v: v.reshape(depth, 1, -1)

    eye = jnp.eye(HEADS, dtype=F32)
    pool_bd = bf((eye[None, :, None, :, None] * pool_w[:, :, :, None, :])
                 .reshape(depth, D_GROUP, D_GROUP))
    bs_full = jnp.repeat(jnp.transpose(gmlp_b_s, (0, 2, 1)), HEAD_DIM, axis=2)
    local_params = (pool_bd, row(pool_scale), conv_w, row(gmlp_ln_g), row(gmlp_ln_b),
                    bf(gmlp_w_s), bs_full)

    ffn_in_f32 = (ffn1_w_gate, ffn1_w_up, ffn1_w_down, mix_w_in)
    mix_ffn_f32 = (ffn2_w_gate, ffn2_w_up, ffn2_w_down, mix_w_out)
    w1g, w1u, w1d, w_in = (bf(w[0]) for w in ffn_in_f32)

    h = x.reshape(bsz * seq, d)
    for l in range(depth):
        x1, a, loc, (w2g, w2u, w2d, w_out) = _ffn_in(
            h, w1g, w1u, w1d, row(ln1_g), row(ln1_b), w_in, mix_ffn_f32, l, alpha=alpha, tm=tm)
        ya = _fourier(a, wc, f1, g2, bsz=bsz, seq=seq)
        h, nxt = _mix_ffn(x1, ya, loc, local_params, w_out, row(ln2_g), row(ln2_b),
                          w2g, w2u, w2d, row(ln3_g), row(ln3_b),
                          ffn_in_f32 if l + 1 < depth else (), l,
                          alpha=alpha, tm=tm, seq=seq)
        if nxt:
            w1g, w1u, w1d, w_in = nxt
    return h.reshape(bsz, seq, d)
```

```python
import functools

import numpy as np
import jax
import jax.numpy as jnp
from jax import lax
from jax.experimental import pallas as pl
from jax.experimental.pallas import tpu as pltpu

F32 = jnp.float32
BF16 = jnp.bfloat16

D_GROUP = 256
HEADS = 4
HEAD_DIM = 64
N_PROJ_GROUPS = 7
N_LOCAL_GROUPS = 6
POOL_WINDOWS = (2, 4, 8, 16)
CHUNK = 128
LN_EPS = 1e-5
FF_CHUNK = 256
HALO = 8
BF16_SUBLANES = 16
DFT_N1 = 64
DFT_N2 = 128

V7X_VMEM_LIMIT_BYTES = 56 * 1024 * 1024


def _resident(shape):
    return pl.BlockSpec(shape, lambda *_: (0,) * len(shape), pipeline_mode=pl.Buffered(1))


def _layer_of(stacked, l):
    blk = (None,) + tuple(stacked.shape[1:])
    idx = (l,) + (0,) * (stacked.ndim - 1)
    return pl.BlockSpec(blk, lambda *_: idx, pipeline_mode=pl.Buffered(1))


def _cast_plan(w, l, n_steps):
    r, c = w.shape[1:]
    steps_per_chunk = 1
    while (r * steps_per_chunk) % n_steps or (r * steps_per_chunk // n_steps) % BF16_SUBLANES:
        steps_per_chunk *= 2
        assert steps_per_chunk <= n_steps, (w.shape, n_steps)
    rows = r * steps_per_chunk // n_steps
    in_spec = pl.BlockSpec((None, rows, c), lambda i: (l, i // steps_per_chunk, 0))
    out_spec = pl.BlockSpec((rows, c), lambda i: (i // steps_per_chunk, 0))
    return in_spec, out_spec, jax.ShapeDtypeStruct((r, c), BF16)


def _cast_chunks(src_refs, dst_refs):
    for src, dst in zip(src_refs, dst_refs):
        dst[...] = src[...].astype(BF16)


def _layer_norm(y, g, b):
    mu = jnp.mean(y, axis=-1, keepdims=True)
    yc = y - mu
    var = jnp.mean(yc * yc, axis=-1, keepdims=True)
    return yc * lax.rsqrt(var + LN_EPS) * g + b


def _swiglu(xb, wg_ref, wu_ref, wd_ref, acc_ref):
    d_ff = wg_ref.shape[1]
    for c in range(d_ff // FF_CHUNK):
        sl = slice(c * FF_CHUNK, (c + 1) * FF_CHUNK)
        g = jnp.dot(xb, wg_ref[:, sl], preferred_element_type=F32)
        u = jnp.dot(xb, wu_ref[:, sl], preferred_element_type=F32)
        h = (g * jax.nn.sigmoid(g) * u).astype(BF16)
        d = jnp.dot(h, wd_ref[sl, :], preferred_element_type=F32)
        if c == 0:
            acc_ref[...] = d
        else:
            acc_ref[...] += d
    return acc_ref[...]


def _ffn_in_body(*refs, alpha, n_cast):
    x_ref, wg_ref, wu_ref, wd_ref, g_ref, b_ref, win_ref = refs[:7]
    cast_src = refs[7:7 + n_cast]
    x1_ref, a_ref, loc_ref = refs[7 + n_cast:10 + n_cast]
    cast_dst = refs[10 + n_cast:10 + 2 * n_cast]
    acc_ref = refs[10 + 2 * n_cast]

    x = x_ref[...]
    ffn = _swiglu(x.astype(BF16), wg_ref, wu_ref, wd_ref, acc_ref)
    x1 = _layer_norm(alpha * x + 0.5 * ffn, g_ref[...], b_ref[...])
    x1_ref[...] = x1
    x1b = x1.astype(BF16)
    a_ref[...] = jnp.dot(x1b, win_ref[:, 0:D_GROUP], preferred_element_type=F32)
    for j in range(N_LOCAL_GROUPS):
        sl = slice((j + 1) * D_GROUP, (j + 2) * D_GROUP)
        loc_ref[j] = jnp.dot(x1b, win_ref[:, sl], preferred_element_type=F32)
    _cast_chunks(cast_src, cast_dst)


def _ffn_in(x, wg, wu, wd, g, b, w_in, next_weights, l, *, alpha, tm):
    n, d = x.shape
    n_steps = n // tm
    row = lambda i: (i, 0)
    plans = [_cast_plan(w, l, n_steps) for w in next_weights]
    out = pl.pallas_call(
        functools.partial(_ffn_in_body, alpha=alpha, n_cast=len(plans)),
        grid=(n_steps,),
        in_specs=[
            pl.BlockSpec((tm, d), row),
            _resident(wg.shape), _resident(wu.shape), _resident(wd.shape),
            _layer_of(g, l), _layer_of(b, l), _resident(w_in.shape),
        ] + [p[0] for p in plans],
        out_specs=[
            pl.BlockSpec((tm, d), row),
            pl.BlockSpec((tm, D_GROUP), row),
            pl.BlockSpec((N_LOCAL_GROUPS, tm, D_GROUP), lambda i: (0, i, 0)),
        ] + [p[1] for p in plans],
        out_shape=[
            jax.ShapeDtypeStruct((n, d), F32),
            jax.ShapeDtypeStruct((n, D_GROUP), F32),
            jax.ShapeDtypeStruct((N_LOCAL_GROUPS, n, D_GROUP), F32),
        ] + [p[2] for p in plans],
        scratch_shapes=[pltpu.VMEM((tm, d), F32)],
        compiler_params=pltpu.CompilerParams(
            dimension_semantics=("arbitrary",), vmem_limit_bytes=V7X_VMEM_LIMIT_BYTES),
        name="ffn_in",
    )(x, wg, wu, wd, g, b, w_in, *next_weights)
    return out[0], out[1], out[2], tuple(out[3:])


def _dft_tables(seq):
    n1, n2 = DFT_N1, DFT_N2
    assert n1 * n2 == seq
    c = np.arange(HEAD_DIM)
    ang = 2.0 * np.pi * ((c[:, None] * c[None, :]) % HEAD_DIM) / HEAD_DIM
    eye = np.eye(HEADS)
    wc = np.concatenate([np.kron(eye, np.cos(ang)), -np.kron(eye, np.sin(ang))], axis=1)
    k = np.arange(n1)
    ang1 = 2.0 * np.pi * ((k[:, None] * k[None, :]) % n1) / n1
    fr, fi = np.cos(ang1), -np.sin(ang1)
    f1 = np.block([[fr, -fi], [fi, fr]])
    k1 = np.arange(n1)[:, None, None]
    k2 = np.arange(n2)[None, :, None]
    m = np.arange(n2)[None, None, :]
    ang2 = 2.0 * np.pi * ((m * (k1 + n1 * k2)) % seq) / seq
    scale = 1.0 / np.sqrt(float(seq) * HEAD_DIM)
    g2 = np.concatenate([np.cos(ang2), np.sin(ang2)], axis=2) * scale
    return wc.astype(np.float32), f1.astype(np.float32), g2.astype(np.float32)


def _fourier_a_body(a_ref, wc_ref, f1_ref, z_ref, u_ref, zs_ref, *, t):
    n1 = DFT_N1
    x = pltpu.einshape("abc->bac", a_ref[...]).reshape(t * n1, D_GROUP)
    uc = jnp.dot(x.astype(BF16), wc_ref[...], preferred_element_type=F32)
    for j in range(t):
        rows = slice(j * n1, (j + 1) * n1)
        sl = slice(j * D_GROUP, (j + 1) * D_GROUP)
        u_ref[0:n1, sl] = uc[rows, :D_GROUP].astype(BF16)
        u_ref[n1:2 * n1, sl] = uc[rows, D_GROUP:].astype(BF16)
    z = jnp.dot(f1_ref[...], u_ref[...], preferred_element_type=F32)
    for part in range(2):
        for j in range(t):
            zs_ref[j] = z[part * n1:(part + 1) * n1, j * D_GROUP:(j + 1) * D_GROUP]
        z_ref[part] = pltpu.einshape("abc->bac", zs_ref[...]).astype(BF16)


def _fourier_b_body(z_ref, g2_ref, y_ref, r_ref, *, kt):
    for k in range(kt):
        zs = jnp.concatenate([z_ref[0, k], z_ref[1, k]], axis=0)
        r_ref[k] = jnp.dot(g2_ref[k], zs, preferred_element_type=F32)
    y_ref[...] = pltpu.einshape("abc->bac", r_ref[...]).astype(BF16)


def _fourier(a, wc, f1, g2, *, bsz, seq, t=16, kt=16):
    n1, n2 = DFT_N1, DFT_N2
    a = a.reshape(bsz, n1, n2, D_GROUP)
    z = pl.pallas_call(
        functools.partial(_fourier_a_body, t=t),
        grid=(bsz, n2 // t),
        in_specs=[
            pl.BlockSpec((None, n1, t, D_GROUP), lambda b, i: (b, 0, i, 0)),
            _resident(wc.shape), _resident(f1.shape),
        ],
        out_specs=pl.BlockSpec((None, 2, n1, t, D_GROUP), lambda b, i: (b, 0, 0, i, 0)),
        out_shape=jax.ShapeDtypeStruct((bsz, 2, n1, n2, D_GROUP), BF16),
        scratch_shapes=[pltpu.VMEM((2 * n1, t * D_GROUP), BF16),
                        pltpu.VMEM((t, n1, D_GROUP), F32)],
        compiler_params=pltpu.CompilerParams(
            dimension_semantics=("arbitrary", "arbitrary"),
            vmem_limit_bytes=V7X_VMEM_LIMIT_BYTES),
        name="fourier_a",
    )(a, wc, f1)
    y = pl.pallas_call(
        functools.partial(_fourier_b_body, kt=kt),
        grid=(bsz, n1 // kt),
        in_specs=[
            pl.BlockSpec((None, 2, kt, n2, D_GROUP), lambda b, i: (b, 0, i, 0, 0)),
            pl.BlockSpec((kt, n2, 2 * n2), lambda b, i: (i, 0, 0)),
        ],
        out_specs=pl.BlockSpec((None, n2, kt, D_GROUP), lambda b, i: (b, 0, i, 0)),
        out_shape=jax.ShapeDtypeStruct((bsz, n2, n1, D_GROUP), BF16),
        scratch_shapes=[pltpu.VMEM((kt, n2, D_GROUP), F32)],
        compiler_params=pltpu.CompilerParams(
            dimension_semantics=("arbitrary", "arbitrary"),
            vmem_limit_bytes=V7X_VMEM_LIMIT_BYTES),
        name="fourier_b",
    )(z, g2)
    return y.reshape(bsz * seq, D_GROUP)


def _shift_rows(x, d):
    return pltpu.roll(x, d % x.shape[0], axis=0)


def _pool_mixer(p, p_prev, p_next, pool_bd, pool_scale, head, t0, seq):
    tl = p.shape[0]
    core = slice(HALO, HALO + tl)
    x = jnp.concatenate([p_prev, p, p_next], axis=0)
    w2 = x + _shift_rows(x, 1)
    w4 = _shift_rows(w2, 1) + _shift_rows(w2, -1)
    w8 = _shift_rows(w4, 2) + _shift_rows(w4, -2)
    w16 = _shift_rows(w8, 4) + _shift_rows(w8, -4)
    win = jnp.where(head == 0, w2[core],
                    jnp.where(head == 1, w4[core],
                              jnp.where(head == 2, w8[core], w16[core])))
    half = jnp.where(head == 0, POOL_WINDOWS[0] // 2,
                     jnp.where(head == 1, POOL_WINDOWS[1] // 2,
                               jnp.where(head == 2, POOL_WINDOWS[2] // 2,
                                         POOL_WINDOWS[3] // 2)))
    t = t0 + lax.broadcasted_iota(jnp.int32, (tl, D_GROUP), 0)
    cnt = jnp.minimum(t + half, seq) - jnp.maximum(t - half, 0)
    pooled = win / cnt.astype(F32) - p
    mixed = jnp.dot(pooled.astype(BF16), pool_bd, preferred_element_type=F32)
    return (mixed * pool_scale).astype(BF16)


def _conv_mixer(gate_b, z, z_prev, z_next, conv_w):
    tl = z.shape[0]
    zz = jnp.concatenate([z_prev, z, z_next], axis=0)
    conv = (conv_w[0:1, :] * _shift_rows(zz, 1) + conv_w[1:2, :] * zz
            + conv_w[2:3, :] * _shift_rows(zz, -1))
    return (gate_b * conv[HALO:HALO + tl]).astype(BF16)


def _gmlp_mixer(u, v, ln_g, ln_b, ws_ref, bs_full, head):
    vn = _layer_norm(v, ln_g, ln_b)
    ws_cat = jnp.concatenate([ws_ref[hd] for hd in range(HEADS)], axis=1)
    out = []
    for c in range(v.shape[0] // CHUNK):
        rows = slice(c * CHUNK, (c + 1) * CHUNK)
        vc = vn[rows]
        stacked = jnp.concatenate(
            [jnp.where(head == hd, vc, 0.0).astype(BF16) for hd in range(HEADS)], axis=0)
        mix = jnp.dot(ws_cat, stacked, preferred_element_type=F32) + bs_full
        out.append((u[rows] * mix).astype(BF16))
    return jnp.concatenate(out, axis=0)


N_MIX_FIXED_INPUTS = 29


def _mix_ffn_body(*refs, alpha, tiles_per_seq, seq, n_cast):
    (x1_ref, ya_ref,
     p_ref, gb_ref, gc_ref, h_ref, u_ref, v_ref,
     pp_ref, pn_ref, gcp_ref, gcn_ref, hp_ref, hn_ref,
     poolw_ref, pscale_ref, convw_ref, lng_ref, lnb_ref, ws_ref, bs_ref,
     wo_ref, g2_ref, b2_ref, wg_ref, wu_ref, wd_ref, g3_ref, b3_ref) = refs[:N_MIX_FIXED_INPUTS]
    cast_src = refs[N_MIX_FIXED_INPUTS:N_MIX_FIXED_INPUTS + n_cast]
    out_ref = refs[N_MIX_FIXED_INPUTS + n_cast]
    cast_dst = refs[N_MIX_FIXED_INPUTS + n_cast + 1:N_MIX_FIXED_INPUTS + 2 * n_cast + 1]
    acc_ref = refs[N_MIX_FIXED_INPUTS + 2 * n_cast + 1]

    tm = x1_ref.shape[0]
    ti = lax.rem(pl.program_id(0), tiles_per_seq)
    keep_prev = (ti > 0).astype(F32)
    keep_next = (ti < tiles_per_seq - 1).astype(F32)
    head = lax.broadcasted_iota(jnp.int32, (1, D_GROUP), 1) // HEAD_DIM

    yb = _pool_mixer(p_ref[...], pp_ref[...] * keep_prev, pn_ref[...] * keep_next,
                     poolw_ref[...], pscale_ref[...], head, ti * tm, seq)
    yc = _conv_mixer(gb_ref[...], gc_ref[...] * h_ref[...],
                     gcp_ref[...] * hp_ref[...] * keep_prev,
                     gcn_ref[...] * hn_ref[...] * keep_next, convw_ref[...])
    yd = _gmlp_mixer(u_ref[...], v_ref[...], lng_ref[...], lnb_ref[...],
                     ws_ref, bs_ref[...], head)

    mix = jnp.dot(ya_ref[...], wo_ref[0:D_GROUP, :], preferred_element_type=F32)
    for j, y in enumerate((yb, yc, yd)):
        mix += jnp.dot(y, wo_ref[(j + 1) * D_GROUP:(j + 2) * D_GROUP, :],
                       preferred_element_type=F32)
    x2 = _layer_norm(alpha * x1_ref[...] + mix, g2_ref[...], b2_ref[...])
    ffn = _swiglu(x2.astype(BF16), wg_ref, wu_ref, wd_ref, acc_ref)
    out_ref[...] = _layer_norm(alpha * x2 + 0.5 * ffn, g3_ref[...], b3_ref[...])
    _cast_chunks(cast_src, cast_dst)


def _mix_ffn(x1, ya, loc, local_params, w_out, g2, b2, wg, wu, wd, g3, b3, next_weights, l,
             *, alpha, tm, seq):
    n, d = x1.shape
    n_steps = n // tm
    tiles_per_seq = seq // tm
    hb = tm // HALO
    last_hb = n // HALO - 1
    row = lambda i: (i, 0)

    def main(g):
        return pl.BlockSpec((None, tm, D_GROUP), lambda i: (g, i, 0))

    def prev(g):
        return pl.BlockSpec((None, HALO, D_GROUP),
                            lambda i: (g, jnp.maximum(i * hb - 1, 0), 0))

    def nxt(g):
        return pl.BlockSpec((None, HALO, D_GROUP),
                            lambda i: (g, jnp.minimum((i + 1) * hb, last_hb), 0))

    plans = [_cast_plan(w, l + 1, n_steps) for w in next_weights]
    in_specs = ([pl.BlockSpec((tm, d), row), pl.BlockSpec((tm, D_GROUP), row),
                 main(0), main(1), main(2), main(3), main(4), main(5),
                 prev(0), nxt(0), prev(2), nxt(2), prev(3), nxt(3)]
                + [_layer_of(w, l) for w in local_params]
                + [_resident(w_out.shape), _layer_of(g2, l), _layer_of(b2, l),
                   _resident(wg.shape), _resident(wu.shape), _resident(wd.shape),
                   _layer_of(g3, l), _layer_of(b3, l)])
    assert len(in_specs) == N_MIX_FIXED_INPUTS
    out = pl.pallas_call(
        functools.partial(_mix_ffn_body, alpha=alpha, tiles_per_seq=tiles_per_seq, seq=seq,
                          n_cast=len(plans)),
        grid=(n_steps,),
        in_specs=in_specs + [p[0] for p in plans],
        out_specs=[pl.BlockSpec((tm, d), row)] + [p[1] for p in plans],
        out_shape=[jax.ShapeDtypeStruct((n, d), F32)] + [p[2] for p in plans],
        scratch_shapes=[pltpu.VMEM((tm, d), F32)],
        compiler_params=pltpu.CompilerParams(
            dimension_semantics=("arbitrary",), vmem_limit_bytes=V7X_VMEM_LIMIT_BYTES),
        name="mix_ffn",
    )(x1, ya, *([loc] * 12), *local_params, w_out, g2, b2, wg, wu, wd, g3, b3, *next_weights)
    return out[0], tuple(out[1:])


def kernel(x, ffn1_w_gate, ffn1_w_up, ffn1_w_down, ln1_g, ln1_b, mix_w_in, pool_w, pool_scale, conv_w, gmlp_ln_g, gmlp_ln_b, gmlp_w_s, gmlp_b_s, mix_w_out, ln2_g, ln2_b, ffn2_w_gate, ffn2_w_up, ffn2_w_down, ln3_g, ln3_b):
    bsz, seq, d = x.shape
    depth = ffn1_w_gate.shape[0]
    alpha = float((2 * depth) ** 0.25)
    tm = 512
    assert d == HEADS * D_GROUP and mix_w_in.shape[2] == N_PROJ_GROUPS * D_GROUP
    assert seq % tm == 0 and tm % CHUNK == 0 and gmlp_w_s.shape[2] == CHUNK

    wc, f1, g2 = (jnp.asarray(tbl).astype(BF16) for tbl in _dft_tables(seq))
    bf = lambda w: w.astype(BF16)
    row = lambda v: v.reshape(depth, 1, -1)

    eye = jnp.eye(HEADS, dtype=F32)
    pool_bd = bf((eye[None, :, None, :, None] * pool_w[:, :, :, None, :])
                 .reshape(depth, D_GROUP, D_GROUP))
    expand = jnp.repeat(eye, HEAD_DIM, axis=1)
    bs_full = jnp.einsum("lhq,hc->lqc", gmlp_b_s, expand, precision=lax.Precision.HIGHEST)
    local_params = (pool_bd, row(pool_scale), conv_w, row(gmlp_ln_g), row(gmlp_ln_b),
                    bf(gmlp_w_s), bs_full)

    ffn_in_f32 = (ffn1_w_gate, ffn1_w_up, ffn1_w_down, mix_w_in)
    mix_ffn_f32 = (ffn2_w_gate, ffn2_w_up, ffn2_w_down, mix_w_out)
    w1g, w1u, w1d, w_in = (bf(w[0]) for w in ffn_in_f32)

    h = x.reshape(bsz * seq, d)
    for l in range(depth):
        x1, a, loc, (w2g, w2u, w2d, w_out) = _ffn_in(
            h, w1g, w1u, w1d, row(ln1_g), row(ln1_b), w_in, mix_ffn_f32, l, alpha=alpha, tm=tm)
        ya = _fourier(a, wc, f1, g2, bsz=bsz, seq=seq)
        h, nxt = _mix_ffn(x1, ya, loc, local_params, w_out, row(ln2_g), row(ln2_b),
                          w2g, w2u, w2d, row(ln3_g), row(ln3_b),
                          ffn_in_f32 if l + 1 < depth else (), l,
                          alpha=alpha, tm=tm, seq=seq)
        if nxt:
            w1g, w1u, w1d, w_in = nxt
    return h.reshape(bsz, seq, d)
```

```python
import functools

import numpy as np
import jax
import jax.numpy as jnp
from jax import lax
from jax.experimental import pallas as pl
from jax.experimental.pallas import tpu as pltpu

F32 = jnp.float32
BF16 = jnp.bfloat16

D_GROUP = 256
HEADS = 4
HEAD_DIM = 64
N_PROJ_GROUPS = 7
N_LOCAL_GROUPS = 6
POOL_WINDOWS = (2, 4, 8, 16)
CHUNK = 128
LN_EPS = 1e-5
FF_CHUNK = 256
HALO = 8
BF16_SUBLANES = 16
DFT_N1 = 64
DFT_N2 = 128

V7X_VMEM_LIMIT_BYTES = 56 * 1024 * 1024


def _resident(shape):
    return pl.BlockSpec(shape, lambda *_: (0,) * len(shape), pipeline_mode=pl.Buffered(1))


def _layer_of(stacked, l):
    blk = (None,) + tuple(stacked.shape[1:])
    idx = (l,) + (0,) * (stacked.ndim - 1)
    return pl.BlockSpec(blk, lambda *_: idx, pipeline_mode=pl.Buffered(1))


def _cast_plan(w, l, n_steps):
    r, c = w.shape[1:]
    steps_per_chunk = 1
    while (r * steps_per_chunk) % n_steps or (r * steps_per_chunk // n_steps) % BF16_SUBLANES:
        steps_per_chunk *= 2
        assert steps_per_chunk <= n_steps, (w.shape, n_steps)
    rows = r * steps_per_chunk // n_steps
    in_spec = pl.BlockSpec((None, rows, c), lambda i: (l, i // steps_per_chunk, 0))
    out_spec = pl.BlockSpec((rows, c), lambda i: (i // steps_per_chunk, 0))
    return in_spec, out_spec, jax.ShapeDtypeStruct((r, c), BF16)


def _cast_chunks(src_refs, dst_refs):
    for src, dst in zip(src_refs, dst_refs):
        dst[...] = src[...].astype(BF16)


def _layer_norm(y, g, b):
    mu = jnp.mean(y, axis=-1, keepdims=True)
    yc = y - mu
    var = jnp.mean(yc * yc, axis=-1, keepdims=True)
    return yc * lax.rsqrt(var + LN_EPS) * g + b


def _swiglu(xb, wg_ref, wu_ref, wd_ref, acc_ref):
    d_ff = wg_ref.shape[1]
    for c in range(d_ff // FF_CHUNK):
        sl = slice(c * FF_CHUNK, (c + 1) * FF_CHUNK)
        g = jnp.dot(xb, wg_ref[:, sl], preferred_element_type=F32)
        u = jnp.dot(xb, wu_ref[:, sl], preferred_element_type=F32)
        h = (g * jax.nn.sigmoid(g) * u).astype(BF16)
        d = jnp.dot(h, wd_ref[sl, :], preferred_element_type=F32)
        if c == 0:
            acc_ref[...] = d
        else:
            acc_ref[...] += d
    return acc_ref[...]


def _ffn_in_body(*refs, alpha, n_cast):
    x_ref, wg_ref, wu_ref, wd_ref, g_ref, b_ref, win_ref = refs[:7]
    cast_src = refs[7:7 + n_cast]
    x1_ref, a_ref, loc_ref = refs[7 + n_cast:10 + n_cast]
    cast_dst = refs[10 + n_cast:10 + 2 * n_cast]
    acc_ref = refs[10 + 2 * n_cast]

    x = x_ref[...]
    ffn = _swiglu(x.astype(BF16), wg_ref, wu_ref, wd_ref, acc_ref)
    x1 = _layer_norm(alpha * x + 0.5 * ffn, g_ref[...], b_ref[...])
    x1_ref[...] = x1
    x1b = x1.astype(BF16)
    a_ref[...] = jnp.dot(x1b, win_ref[:, 0:D_GROUP], preferred_element_type=F32)
    for j in range(N_LOCAL_GROUPS):
        sl = slice((j + 1) * D_GROUP, (j + 2) * D_GROUP)
        loc_ref[j] = jnp.dot(x1b, win_ref[:, sl], preferred_element_type=F32)
    _cast_chunks(cast_src, cast_dst)


def _ffn_in(x, wg, wu, wd, g, b, w_in, next_weights, l, *, alpha, tm):
    n, d = x.shape
    n_steps = n // tm
    row = lambda i: (i, 0)
    plans = [_cast_plan(w, l, n_steps) for w in next_weights]
    out = pl.pallas_call(
        functools.partial(_ffn_in_body, alpha=alpha, n_cast=len(plans)),
        grid=(n_steps,),
        in_specs=[
            pl.BlockSpec((tm, d), row),
            _resident(wg.shape), _resident(wu.shape), _resident(wd.shape),
            _layer_of(g, l), _layer_of(b, l), _resident(w_in.shape),
        ] + [p[0] for p in plans],
        out_specs=[
            pl.BlockSpec((tm, d), row),
            pl.BlockSpec((tm, D_GROUP), row),
            pl.BlockSpec((N_LOCAL_GROUPS, tm, D_GROUP), lambda i: (0, i, 0)),
        ] + [p[1] for p in plans],
        out_shape=[
            jax.ShapeDtypeStruct((n, d), F32),
            jax.ShapeDtypeStruct((n, D_GROUP), F32),
            jax.ShapeDtypeStruct((N_LOCAL_GROUPS, n, D_GROUP), F32),
        ] + [p[2] for p in plans],
        scratch_shapes=[pltpu.VMEM((tm, d), F32)],
        compiler_params=pltpu.CompilerParams(
            dimension_semantics=("arbitrary",), vmem_limit_bytes=V7X_VMEM_LIMIT_BYTES),
        name="ffn_in",
    )(x, wg, wu, wd, g, b, w_in, *next_weights)
    return out[0], out[1], out[2], tuple(out[3:])


def _dft_tables(seq):
    n1, n2 = DFT_N1, DFT_N2
    assert n1 * n2 == seq
    c = np.arange(HEAD_DIM)
    ang = 2.0 * np.pi * ((c[:, None] * c[None, :]) % HEAD_DIM) / HEAD_DIM
    eye = np.eye(HEADS)
    wc = np.concatenate([np.kron(eye, np.cos(ang)), -np.kron(eye, np.sin(ang))], axis=1)
    k = np.arange(n1)
    ang1 = 2.0 * np.pi * ((k[:, None] * k[None, :]) % n1) / n1
    fr, fi = np.cos(ang1), -np.sin(ang1)
    f1 = np.block([[fr, -fi], [fi, fr]])
    k1 = np.arange(n1)[:, None, None]
    k2 = np.arange(n2)[None, :, None]
    m = np.arange(n2)[None, None, :]
    ang2 = 2.0 * np.pi * ((m * (k1 + n1 * k2)) % seq) / seq
    scale = 1.0 / np.sqrt(float(seq) * HEAD_DIM)
    g2 = np.concatenate([np.cos(ang2), np.sin(ang2)], axis=2) * scale
    return wc.astype(np.float32), f1.astype(np.float32), g2.astype(np.float32)


def _swap_major_sublane(x):
    return jnp.transpose(x, (1, 0, 2))


def _fourier_a_body(a_ref, wc_ref, f1_ref, z_ref, u_ref, zs_ref, *, t):
    n1 = DFT_N1
    x = _swap_major_sublane(a_ref[...]).reshape(t * n1, D_GROUP)
    uc = jnp.dot(x.astype(BF16), wc_ref[...], preferred_element_type=F32)
    for j in range(t):
        rows = slice(j * n1, (j + 1) * n1)
        sl = slice(j * D_GROUP, (j + 1) * D_GROUP)
        u_ref[0:n1, sl] = uc[rows, :D_GROUP].astype(BF16)
        u_ref[n1:2 * n1, sl] = uc[rows, D_GROUP:].astype(BF16)
    z = jnp.dot(f1_ref[...], u_ref[...], preferred_element_type=F32)
    for part in range(2):
        for j in range(t):
            zs_ref[j] = z[part * n1:(part + 1) * n1, j * D_GROUP:(j + 1) * D_GROUP]
        z_ref[part] = _swap_major_sublane(zs_ref[...]).astype(BF16)


def _fourier_b_body(z_ref, g2_ref, y_ref, r_ref, *, kt):
    for k in range(kt):
        zs = jnp.concatenate([z_ref[0, k], z_ref[1, k]], axis=0)
        r_ref[k] = jnp.dot(g2_ref[k], zs, preferred_element_type=F32)
    y_ref[...] = _swap_major_sublane(r_ref[...]).astype(BF16)


def _fourier(a, wc, f1, g2, *, bsz, seq, t=16, kt=16):
    n1, n2 = DFT_N1, DFT_N2
    a = a.reshape(bsz, n1, n2, D_GROUP)
    z = pl.pallas_call(
        functools.partial(_fourier_a_body, t=t),
        grid=(bsz, n2 // t),
        in_specs=[
            pl.BlockSpec((None, n1, t, D_GROUP), lambda b, i: (b, 0, i, 0)),
            _resident(wc.shape), _resident(f1.shape),
        ],
        out_specs=pl.BlockSpec((None, 2, n1, t, D_GROUP), lambda b, i: (b, 0, 0, i, 0)),
        out_shape=jax.ShapeDtypeStruct((bsz, 2, n1, n2, D_GROUP), BF16),
        scratch_shapes=[pltpu.VMEM((2 * n1, t * D_GROUP), BF16),
                        pltpu.VMEM((t, n1, D_GROUP), F32)],
        compiler_params=pltpu.CompilerParams(
            dimension_semantics=("arbitrary", "arbitrary"),
            vmem_limit_bytes=V7X_VMEM_LIMIT_BYTES),
        name="fourier_a",
    )(a, wc, f1)
    y = pl.pallas_call(
        functools.partial(_fourier_b_body, kt=kt),
        grid=(bsz, n1 // kt),
        in_specs=[
            pl.BlockSpec((None, 2, kt, n2, D_GROUP), lambda b, i: (b, 0, i, 0, 0)),
            pl.BlockSpec((kt, n2, 2 * n2), lambda b, i: (i, 0, 0)),
        ],
        out_specs=pl.BlockSpec((None, n2, kt, D_GROUP), lambda b, i: (b, 0, i, 0)),
        out_shape=jax.ShapeDtypeStruct((bsz, n2, n1, D_GROUP), BF16),
        scratch_shapes=[pltpu.VMEM((kt, n2, D_GROUP), F32)],
        compiler_params=pltpu.CompilerParams(
            dimension_semantics=("arbitrary", "arbitrary"),
            vmem_limit_bytes=V7X_VMEM_LIMIT_BYTES),
        name="fourier_b",
    )(z, g2)
    return y.reshape(bsz * seq, D_GROUP)


def _shift_rows(x, d):
    return pltpu.roll(x, d % x.shape[0], axis=0)


def _pool_mixer(p, p_prev, p_next, pool_bd, pool_scale, head, t0, seq):
    tl = p.shape[0]
    core = slice(HALO, HALO + tl)
    x = jnp.concatenate([p_prev, p, p_next], axis=0)
    w2 = x + _shift_rows(x, 1)
    w4 = _shift_rows(w2, 1) + _shift_rows(w2, -1)
    w8 = _shift_rows(w4, 2) + _shift_rows(w4, -2)
    w16 = _shift_rows(w8, 4) + _shift_rows(w8, -4)
    win = jnp.where(head == 0, w2[core],
                    jnp.where(head == 1, w4[core],
                              jnp.where(head == 2, w8[core], w16[core])))
    half = jnp.where(head == 0, POOL_WINDOWS[0] // 2,
                     jnp.where(head == 1, POOL_WINDOWS[1] // 2,
                               jnp.where(head == 2, POOL_WINDOWS[2] // 2,
                                         POOL_WINDOWS[3] // 2)))
    t = t0 + lax.broadcasted_iota(jnp.int32, (tl, D_GROUP), 0)
    cnt = jnp.minimum(t + half, seq) - jnp.maximum(t - half, 0)
    pooled = win / cnt.astype(F32) - p
    mixed = jnp.dot(pooled.astype(BF16), pool_bd, preferred_element_type=F32)
    return (mixed * pool_scale).astype(BF16)


def _conv_mixer(gate_b, z, z_prev, z_next, conv_w):
    tl = z.shape[0]
    zz = jnp.concatenate([z_prev, z, z_next], axis=0)
    conv = (conv_w[0:1, :] * _shift_rows(zz, 1) + conv_w[1:2, :] * zz
            + conv_w[2:3, :] * _shift_rows(zz, -1))
    return (gate_b * conv[HALO:HALO + tl]).astype(BF16)


def _gmlp_mixer(u, v, ln_g, ln_b, ws_ref, bs_full, head):
    vn = _layer_norm(v, ln_g, ln_b)
    ws_cat = jnp.concatenate([ws_ref[hd] for hd in range(HEADS)], axis=1)
    out = []
    for c in range(v.shape[0] // CHUNK):
        rows = slice(c * CHUNK, (c + 1) * CHUNK)
        vc = vn[rows]
        stacked = jnp.concatenate(
            [jnp.where(head == hd, vc, 0.0).astype(BF16) for hd in range(HEADS)], axis=0)
        mix = jnp.dot(ws_cat, stacked, preferred_element_type=F32) + bs_full
        out.append((u[rows] * mix).astype(BF16))
    return jnp.concatenate(out, axis=0)


N_MIX_FIXED_INPUTS = 29


def _mix_ffn_body(*refs, alpha, tiles_per_seq, seq, n_cast):
    (x1_ref, ya_ref,
     p_ref, gb_ref, gc_ref, h_ref, u_ref, v_ref,
     pp_ref, pn_ref, gcp_ref, gcn_ref, hp_ref, hn_ref,
     poolw_ref, pscale_ref, convw_ref, lng_ref, lnb_ref, ws_ref, bs_ref,
     wo_ref, g2_ref, b2_ref, wg_ref, wu_ref, wd_ref, g3_ref, b3_ref) = refs[:N_MIX_FIXED_INPUTS]
    cast_src = refs[N_MIX_FIXED_INPUTS:N_MIX_FIXED_INPUTS + n_cast]
    out_ref = refs[N_MIX_FIXED_INPUTS + n_cast]
    cast_dst = refs[N_MIX_FIXED_INPUTS + n_cast + 1:N_MIX_FIXED_INPUTS + 2 * n_cast + 1]
    acc_ref = refs[N_MIX_FIXED_INPUTS + 2 * n_cast + 1]

    tm = x1_ref.shape[0]
    ti = lax.rem(pl.program_id(0), tiles_per_seq)
    keep_prev = (ti > 0).astype(F32)
    keep_next = (ti < tiles_per_seq - 1).astype(F32)
    head = lax.broadcasted_iota(jnp.int32, (1, D_GROUP), 1) // HEAD_DIM

    yb = _pool_mixer(p_ref[...], pp_ref[...] * keep_prev, pn_ref[...] * keep_next,
                     poolw_ref[...], pscale_ref[...], head, ti * tm, seq)
    yc = _conv_mixer(gb_ref[...], gc_ref[...] * h_ref[...],
                     gcp_ref[...] * hp_ref[...] * keep_prev,
                     gcn_ref[...] * hn_ref[...] * keep_next, convw_ref[...])
    yd = _gmlp_mixer(u_ref[...], v_ref[...], lng_ref[...], lnb_ref[...],
                     ws_ref, bs_ref[...], head)

    mix = jnp.dot(ya_ref[...], wo_ref[0:D_GROUP, :], preferred_element_type=F32)
    for j, y in enumerate((yb, yc, yd)):
        mix += jnp.dot(y, wo_ref[(j + 1) * D_GROUP:(j + 2) * D_GROUP, :],
                       preferred_element_type=F32)
    x2 = _layer_norm(alpha * x1_ref[...] + mix, g2_ref[...], b2_ref[...])
    ffn = _swiglu(x2.astype(BF16), wg_ref, wu_ref, wd_ref, acc_ref)
    out_ref[...] = _layer_norm(alpha * x2 + 0.5 * ffn, g3_ref[...], b3_ref[...])
    _cast_chunks(cast_src, cast_dst)


def _mix_ffn(x1, ya, loc, local_params, w_out, g2, b2, wg, wu, wd, g3, b3, next_weights, l,
             *, alpha, tm, seq):
    n, d = x1.shape
    n_steps = n // tm
    tiles_per_seq = seq // tm
    hb = tm // HALO
    last_hb = n // HALO - 1
    row = lambda i: (i, 0)

    def main(g):
        return pl.BlockSpec((None, tm, D_GROUP), lambda i: (g, i, 0))

    def prev(g):
        return pl.BlockSpec((None, HALO, D_GROUP),
                            lambda i: (g, jnp.maximum(i * hb - 1, 0), 0))

    def nxt(g):
        return pl.BlockSpec((None, HALO, D_GROUP),
                            lambda i: (g, jnp.minimum((i + 1) * hb, last_hb), 0))

    plans = [_cast_plan(w, l + 1, n_steps) for w in next_weights]
    in_specs = ([pl.BlockSpec((tm, d), row), pl.BlockSpec((tm, D_GROUP), row),
                 main(0), main(1), main(2), main(3), main(4), main(5),
                 prev(0), nxt(0), prev(2), nxt(2), prev(3), nxt(3)]
                + [_layer_of(w, l) for w in local_params]
                + [_resident(w_out.shape), _layer_of(g2, l), _layer_of(b2, l),
                   _resident(wg.shape), _resident(wu.shape), _resident(wd.shape),
                   _layer_of(g3, l), _layer_of(b3, l)])
    assert len(in_specs) == N_MIX_FIXED_INPUTS
    out = pl.pallas_call(
        functools.partial(_mix_ffn_body, alpha=alpha, tiles_per_seq=tiles_per_seq, seq=seq,
                          n_cast=len(plans)),
        grid=(n_steps,),
        in_specs=in_specs + [p[0] for p in plans],
        out_specs=[pl.BlockSpec((tm, d), row)] + [p[1] for p in plans],
        out_shape=[jax.ShapeDtypeStruct((n, d), F32)] + [p[2] for p in plans],
        scratch_shapes=[pltpu.VMEM((tm, d), F32)],
        compiler_params=pltpu.CompilerParams(
            dimension_semantics=("arbitrary",), vmem_limit_bytes=V7X_VMEM_LIMIT_BYTES),
        name="mix_ffn",
    )(x1, ya, *([loc] * 12), *local_params, w_out, g2, b2, wg, wu, wd, g3, b3, *next_weights)
    return out[0], tuple(out[1:])


def kernel(x, ffn1_w_gate, ffn1_w_up, ffn1_w_down, ln1_g, ln1_b, mix_w_in, pool_w, pool_scale, conv_w, gmlp_ln_g, gmlp_ln_b, gmlp_w_s, gmlp_b_s, mix_w_out, ln2_g, ln2_b, ffn2_w_gate, ffn2_w_up, ffn2_w_down, ln3_g, ln3_b):
    bsz, seq, d = x.shape
    depth = ffn1_w_gate.shape[0]
    alpha = float((2 * depth) ** 0.25)
    tm = 512
    assert d == HEADS * D_GROUP and mix_w_in.shape[2] == N_PROJ_GROUPS * D_GROUP
    assert seq % tm == 0 and tm % CHUNK == 0 and gmlp_w_s.shape[2] == CHUNK

    wc, f1, g2 = (jnp.asarray(tbl).astype(BF16) for tbl in _dft_tables(seq))
    bf = lambda w: w.astype(BF16)
    row = lambda v: v.reshape(depth, 1, -1)

    eye = jnp.eye(HEADS, dtype=F32)
    pool_bd = bf((eye[None, :, None, :, None] * pool_w[:, :, :, None, :])
                 .reshape(depth, D_GROUP, D_GROUP))
    expand = jnp.repeat(eye, HEAD_DIM, axis=1)
    bs_full = jnp.einsum("lhq,hc->lqc", gmlp_b_s, expand, precision=lax.Precision.HIGHEST)
    local_params = (pool_bd, row(pool_scale), conv_w, row(gmlp_ln_g), row(gmlp_ln_b),
                    bf(gmlp_w_s), bs_full)

    ffn_in_f32 = (ffn1_w_gate, ffn1_w_up, ffn1_w_down, mix_w_in)
    mix_ffn_f32 = (ffn2_w_gate, ffn2_w_up, ffn2_w_down, mix_w_out)
    w1g, w1u, w1d, w_in = (bf(w[0]) for w in ffn_in_f32)

    h = x.reshape(bsz * seq, d)
    for l in range(depth):
        x1, a, loc, (w2g, w2u, w2d, w_out) = _ffn_in(
            h, w1g, w1u, w1d, row(ln1_g), row(ln1_b), w_in, mix_ffn_f32, l, alpha=alpha, tm=tm)
        ya = _fourier(a, wc, f1, g2, bsz=bsz, seq=seq)
        h, nxt = _mix_ffn(x1, ya, loc, local_params, w_out, row(ln2_g), row(ln2_b),
                          w2g, w2u, w2d, row(ln3_g), row(ln3_b),
                          ffn_in_f32 if l + 1 < depth else (), l,
                          alpha=alpha, tm=tm, seq=seq)
        if nxt:
            w1g, w1u, w1d, w_in = nxt
    return h.reshape(bsz, seq, d)
```

```python
import functools

import numpy as np
import jax
import jax.numpy as jnp
from jax import lax
from jax.experimental import pallas as pl
from jax.experimental.pallas import tpu as pltpu

F32 = jnp.float32
BF16 = jnp.bfloat16

D_GROUP = 256
HEADS = 4
HEAD_DIM = 64
N_PROJ_GROUPS = 7
N_LOCAL_GROUPS = 6
POOL_WINDOWS = (2, 4, 8, 16)
CHUNK = 128
LN_EPS = 1e-5
FF_CHUNK = 256
HALO = 8
BF16_SUBLANES = 16
DFT_N1 = 64
DFT_N2 = 128

V7X_VMEM_LIMIT_BYTES = 56 * 1024 * 1024


def _resident(shape):
    return pl.BlockSpec(shape, lambda *_: (0,) * len(shape), pipeline_mode=pl.Buffered(1))


def _layer_of(stacked, l):
    blk = (None,) + tuple(stacked.shape[1:])
    idx = (l,) + (0,) * (stacked.ndim - 1)
    return pl.BlockSpec(blk, lambda *_: idx, pipeline_mode=pl.Buffered(1))


def _cast_plan(w, l, n_steps):
    r, c = w.shape[1:]
    steps_per_chunk = 1
    while (r * steps_per_chunk) % n_steps or (r * steps_per_chunk // n_steps) % BF16_SUBLANES:
        steps_per_chunk *= 2
        assert steps_per_chunk <= n_steps, (w.shape, n_steps)
    rows = r * steps_per_chunk // n_steps
    chunk = lambda i: jnp.minimum(i, n_steps - 1) // steps_per_chunk
    in_spec = pl.BlockSpec((None, rows, c), lambda i: (l, chunk(i), 0))
    out_spec = pl.BlockSpec((rows, c), lambda i: (chunk(i), 0))
    return in_spec, out_spec, jax.ShapeDtypeStruct((r, c), BF16)


def _cast_chunks(src_refs, dst_refs):
    for src, dst in zip(src_refs, dst_refs):
        dst[...] = src[...].astype(BF16)


def _layer_norm(y, g, b):
    mu = jnp.mean(y, axis=-1, keepdims=True)
    yc = y - mu
    var = jnp.mean(yc * yc, axis=-1, keepdims=True)
    return yc * lax.rsqrt(var + LN_EPS) * g + b


def _swiglu(xb, wg_ref, wu_ref, wd_ref, acc_ref, finish, other_work=None):
    n_chunks = wg_ref.shape[1] // FF_CHUNK
    other_work = other_work or {}
    assert all(0 <= c < n_chunks for c in other_work)

    def gate_up(c):
        sl = slice(c * FF_CHUNK, (c + 1) * FF_CHUNK)
        return (jnp.dot(xb, wg_ref[:, sl], preferred_element_type=F32),
                jnp.dot(xb, wu_ref[:, sl], preferred_element_type=F32))

    g, u = gate_up(0)
    for c in range(n_chunks):
        last = c == n_chunks - 1
        nxt = None if last else gate_up(c + 1)
        if c in other_work:
            other_work[c]()
        h = (g * jax.nn.sigmoid(g) * u).astype(BF16)
        d = jnp.dot(h, wd_ref[c * FF_CHUNK:(c + 1) * FF_CHUNK, :], preferred_element_type=F32)
        if c == 0:
            acc_ref[...] = d
        elif not last:
            acc_ref[...] += d
        else:
            finish(acc_ref[...] + d)
        if not last:
            g, u = nxt


def _ffn_in_body(*refs, alpha, n_cast, n_tiles):
    x_ref, xp_ref, wg_ref, wu_ref, wd_ref, g_ref, b_ref, win_ref = refs[:8]
    cast_src = refs[8:8 + n_cast]
    x1_ref, a_ref, loc_ref = refs[8 + n_cast:11 + n_cast]
    cast_dst = refs[11 + n_cast:11 + 2 * n_cast]
    acc_ref, ffn_ref, x1b_ref = refs[11 + 2 * n_cast:]
    s = pl.program_id(0)
    last_chunk = wg_ref.shape[1] // FF_CHUNK - 1

    @pl.when(s == 0)
    def _():
        ffn_ref[...] = jnp.zeros_like(ffn_ref)

    def layer_norm_prev():
        x1 = _layer_norm(alpha * xp_ref[...] + 0.5 * ffn_ref[...], g_ref[...], b_ref[...])
        x1_ref[...] = x1
        x1b_ref[...] = x1.astype(BF16)
        _cast_chunks(cast_src, cast_dst)

    def project_prev():
        x1b = x1b_ref[...]
        a_ref[...] = jnp.dot(x1b, win_ref[:, 0:D_GROUP], preferred_element_type=F32)
        for j in range(N_LOCAL_GROUPS):
            sl = slice((j + 1) * D_GROUP, (j + 2) * D_GROUP)
            loc_ref[j] = jnp.dot(x1b, win_ref[:, sl], preferred_element_type=F32)

    def store_ffn(total):
        ffn_ref[...] = total

    @pl.when(s < n_tiles)
    def _():
        _swiglu(x_ref[...].astype(BF16), wg_ref, wu_ref, wd_ref, acc_ref, store_ffn,
                other_work={0: layer_norm_prev, last_chunk: project_prev})

    @pl.when(s == n_tiles)
    def _():
        layer_norm_prev()
        project_prev()


def _ffn_in(x, wg, wu, wd, g, b, w_in, next_weights, l, *, alpha, tm):
    n, d = x.shape
    n_tiles = n // tm
    cur = lambda s: (jnp.minimum(s, n_tiles - 1), 0)
    prv = lambda s: (jnp.maximum(s - 1, 0), 0)
    plans = [_cast_plan(w, l, n_tiles) for w in next_weights]
    out = pl.pallas_call(
        functools.partial(_ffn_in_body, alpha=alpha, n_cast=len(plans), n_tiles=n_tiles),
        grid=(n_tiles + 1,),
        in_specs=[
            pl.BlockSpec((tm, d), cur), pl.BlockSpec((tm, d), prv),
            _resident(wg.shape), _resident(wu.shape), _resident(wd.shape),
            _layer_of(g, l), _layer_of(b, l), _resident(w_in.shape),
        ] + [p[0] for p in plans],
        out_specs=[
            pl.BlockSpec((tm, d), prv),
            pl.BlockSpec((tm, D_GROUP), prv),
            pl.BlockSpec((N_LOCAL_GROUPS, tm, D_GROUP), lambda s: (0, jnp.maximum(s - 1, 0), 0)),
        ] + [p[1] for p in plans],
        out_shape=[
            jax.ShapeDtypeStruct((n, d), F32),
            jax.ShapeDtypeStruct((n, D_GROUP), F32),
            jax.ShapeDtypeStruct((N_LOCAL_GROUPS, n, D_GROUP), F32),
        ] + [p[2] for p in plans],
        scratch_shapes=[pltpu.VMEM((tm, d), F32), pltpu.VMEM((tm, d), F32),
                        pltpu.VMEM((tm, d), BF16)],
        compiler_params=pltpu.CompilerParams(
            dimension_semantics=("arbitrary",), vmem_limit_bytes=V7X_VMEM_LIMIT_BYTES),
        name="ffn_in",
    )(x, x, wg, wu, wd, g, b, w_in, *next_weights)
    return out[0], out[1], out[2], tuple(out[3:])


def _dft_tables(seq):
    n1, n2 = DFT_N1, DFT_N2
    assert n1 * n2 == seq
    c = np.arange(HEAD_DIM)
    ang = 2.0 * np.pi * ((c[:, None] * c[None, :]) % HEAD_DIM) / HEAD_DIM
    eye = np.eye(HEADS)
    wc = np.concatenate([np.kron(eye, np.cos(ang)), -np.kron(eye, np.sin(ang))], axis=1)
    k = np.arange(n1)
    ang1 = 2.0 * np.pi * ((k[:, None] * k[None, :]) % n1) / n1
    fr, fi = np.cos(ang1), -np.sin(ang1)
    f1 = np.block([[fr, -fi], [fi, fr]])
    k1 = np.arange(n1)[:, None, None]
    k2 = np.arange(n2)[None, :, None]
    m = np.arange(n2)[None, None, :]
    ang2 = 2.0 * np.pi * ((m * (k1 + n1 * k2)) % seq) / seq
    scale = 1.0 / np.sqrt(float(seq) * HEAD_DIM)
    g2 = np.concatenate([np.cos(ang2), np.sin(ang2)], axis=2) * scale
    return wc.astype(np.float32), f1.astype(np.float32), g2.astype(np.float32)


def _swap_major_sublane(x):
    return jnp.transpose(x, (1, 0, 2))


def _fourier_a_body(a_ref, wc_ref, f1_ref, z_ref, u_ref, zs_ref, *, t):
    n1 = DFT_N1
    x = _swap_major_sublane(a_ref[...]).reshape(t * n1, D_GROUP)
    uc = jnp.dot(x.astype(BF16), wc_ref[...], preferred_element_type=F32)
    for j in range(t):
        rows = slice(j * n1, (j + 1) * n1)
        sl = slice(j * D_GROUP, (j + 1) * D_GROUP)
        u_ref[0:n1, sl] = uc[rows, :D_GROUP].astype(BF16)
        u_ref[n1:2 * n1, sl] = uc[rows, D_GROUP:].astype(BF16)
    z = jnp.dot(f1_ref[...], u_ref[...], preferred_element_type=F32)
    for part in range(2):
        for j in range(t):
            zs_ref[j] = z[part * n1:(part + 1) * n1, j * D_GROUP:(j + 1) * D_GROUP]
        z_ref[part] = _swap_major_sublane(zs_ref[...]).astype(BF16)


def _fourier_b_body(z_ref, g2_ref, y_ref, r_ref, *, kt):
    for k in range(kt):
        zs = jnp.concatenate([z_ref[0, k], z_ref[1, k]], axis=0)
        r_ref[k] = jnp.dot(g2_ref[k], zs, preferred_element_type=F32)
    y_ref[...] = _swap_major_sublane(r_ref[...]).astype(BF16)


def _fourier(a, wc, f1, g2, *, bsz, seq, t=16, kt=16):
    n1, n2 = DFT_N1, DFT_N2
    a = a.reshape(bsz, n1, n2, D_GROUP)
    z = pl.pallas_call(
        functools.partial(_fourier_a_body, t=t),
        grid=(bsz, n2 // t),
        in_specs=[
            pl.BlockSpec((None, n1, t, D_GROUP), lambda b, i: (b, 0, i, 0)),
            _resident(wc.shape), _resident(f1.shape),
        ],
        out_specs=pl.BlockSpec((None, 2, n1, t, D_GROUP), lambda b, i: (b, 0, 0, i, 0)),
        out_shape=jax.ShapeDtypeStruct((bsz, 2, n1, n2, D_GROUP), BF16),
        scratch_shapes=[pltpu.VMEM((2 * n1, t * D_GROUP), BF16),
                        pltpu.VMEM((t, n1, D_GROUP), F32)],
        compiler_params=pltpu.CompilerParams(
            dimension_semantics=("arbitrary", "arbitrary"),
            vmem_limit_bytes=V7X_VMEM_LIMIT_BYTES),
        name="fourier_a",
    )(a, wc, f1)
    y = pl.pallas_call(
        functools.partial(_fourier_b_body, kt=kt),
        grid=(bsz, n1 // kt),
        in_specs=[
            pl.BlockSpec((None, 2, kt, n2, D_GROUP), lambda b, i: (b, 0, i, 0, 0)),
            pl.BlockSpec((kt, n2, 2 * n2), lambda b, i: (i, 0, 0)),
        ],
        out_specs=pl.BlockSpec((None, n2, kt, D_GROUP), lambda b, i: (b, 0, i, 0)),
        out_shape=jax.ShapeDtypeStruct((bsz, n2, n1, D_GROUP), BF16),
        scratch_shapes=[pltpu.VMEM((kt, n2, D_GROUP), F32)],
        compiler_params=pltpu.CompilerParams(
            dimension_semantics=("arbitrary", "arbitrary"),
            vmem_limit_bytes=V7X_VMEM_LIMIT_BYTES),
        name="fourier_b",
    )(z, g2)
    return y.reshape(bsz * seq, D_GROUP)


def _shift_rows(x, d):
    return pltpu.roll(x, d % x.shape[0], axis=0)


def _pool_windows(p, p_prev, p_next, head, t0, seq):
    tl = p.shape[0]
    core = slice(HALO, HALO + tl)
    x = jnp.concatenate([p_prev, p, p_next], axis=0)
    w2 = x + _shift_rows(x, 1)
    w4 = _shift_rows(w2, 1) + _shift_rows(w2, -1)
    w8 = _shift_rows(w4, 2) + _shift_rows(w4, -2)
    w16 = _shift_rows(w8, 4) + _shift_rows(w8, -4)
    win = jnp.where(head == 0, w2[core],
                    jnp.where(head == 1, w4[core],
                              jnp.where(head == 2, w8[core], w16[core])))
    half = jnp.where(head == 0, POOL_WINDOWS[0] // 2,
                     jnp.where(head == 1, POOL_WINDOWS[1] // 2,
                               jnp.where(head == 2, POOL_WINDOWS[2] // 2,
                                         POOL_WINDOWS[3] // 2)))
    t = t0 + lax.broadcasted_iota(jnp.int32, (tl, D_GROUP), 0)
    cnt = jnp.minimum(t + half, seq) - jnp.maximum(t - half, 0)
    return (win / cnt.astype(F32) - p).astype(BF16)


def _conv_mixer(gate_b, z, z_prev, z_next, conv_w):
    tl = z.shape[0]
    zz = jnp.concatenate([z_prev, z, z_next], axis=0)
    conv = (conv_w[0:1, :] * _shift_rows(zz, 1) + conv_w[1:2, :] * zz
            + conv_w[2:3, :] * _shift_rows(zz, -1))
    return (gate_b * conv[HALO:HALO + tl]).astype(BF16)


def _gmlp_operands(v, ln_g, ln_b, head):
    vn = _layer_norm(v, ln_g, ln_b)
    return [jnp.concatenate([jnp.where(head == hd, vn[c * CHUNK:(c + 1) * CHUNK], 0.0).astype(BF16)
                             for hd in range(HEADS)], axis=0)
            for c in range(v.shape[0] // CHUNK)]


N_MIX_FIXED_INPUTS = 29


def _mix_ffn_body(*refs, alpha, tiles_per_seq, seq, n_cast, n_tiles):
    (x1_ref, ya_ref,
     p_ref, gb_ref, gc_ref, h_ref, u_ref, v_ref,
     pp_ref, pn_ref, gcp_ref, gcn_ref, hp_ref, hn_ref,
     poolw_ref, pscale_ref, convw_ref, lng_ref, lnb_ref, ws_ref, bs_ref,
     wo_ref, g2_ref, b2_ref, wg_ref, wu_ref, wd_ref, g3_ref, b3_ref) = refs[:N_MIX_FIXED_INPUTS]
    cast_src = refs[N_MIX_FIXED_INPUTS:N_MIX_FIXED_INPUTS + n_cast]
    out_ref = refs[N_MIX_FIXED_INPUTS + n_cast]
    cast_dst = refs[N_MIX_FIXED_INPUTS + n_cast + 1:N_MIX_FIXED_INPUTS + 2 * n_cast + 1]
    acc_ref, y3_ref, base_ref, x2_ref, x2b_ref, xq_ref = refs[N_MIX_FIXED_INPUTS + 2 * n_cast + 1:]

    tm = x1_ref.shape[0]
    s = pl.program_id(0)
    ti = lax.rem(jnp.minimum(s, n_tiles - 1), tiles_per_seq)
    keep_prev = (ti > 0).astype(F32)
    keep_next = (ti < tiles_per_seq - 1).astype(F32)
    head = lax.broadcasted_iota(jnp.int32, (1, D_GROUP), 1) // HEAD_DIM
    st = {}

    @pl.when(s == 0)
    def _():
        y3_ref[...] = jnp.zeros_like(y3_ref)

    def layer_norm3():
        out_ref[...] = _layer_norm(y3_ref[...], g3_ref[...], b3_ref[...])
        _cast_chunks(cast_src, cast_dst)

    def mixers_vpu():
        st["pooled"] = _pool_windows(p_ref[...], pp_ref[...] * keep_prev,
                                     pn_ref[...] * keep_next, head, ti * tm, seq)
        st["yc"] = _conv_mixer(gb_ref[...], gc_ref[...] * h_ref[...],
                               gcp_ref[...] * hp_ref[...] * keep_prev,
                               gcn_ref[...] * hn_ref[...] * keep_next, convw_ref[...])
        st["gmlp_in"] = _gmlp_operands(v_ref[...], lng_ref[...], lnb_ref[...], head)

    def mixers_mxu():
        st["pool_mixed"] = jnp.dot(st["pooled"], poolw_ref[...], preferred_element_type=F32)
        ws_cat = jnp.concatenate([ws_ref[hd] for hd in range(HEADS)], axis=1)
        st["gmlp_mixed"] = [jnp.dot(ws_cat, op, preferred_element_type=F32)
                            for op in st["gmlp_in"]]

    def mixers_finish():
        st["yb"] = (st["pool_mixed"] * pscale_ref[...]).astype(BF16)
        st["yd"] = jnp.concatenate(
            [(u_ref[c * CHUNK:(c + 1) * CHUNK, :] * (m + bs_ref[...])).astype(BF16)
             for c, m in enumerate(st["gmlp_mixed"])], axis=0)

    def out_projection():
        mix = jnp.dot(ya_ref[...], wo_ref[0:D_GROUP, :], preferred_element_type=F32)
        for j, y in enumerate((st["yb"], st["yc"], st["yd"])):
            mix += jnp.dot(y, wo_ref[(j + 1) * D_GROUP:(j + 2) * D_GROUP, :],
                           preferred_element_type=F32)
        st["mix"] = mix

    def layer_norm2():
        x2 = _layer_norm(alpha * x1_ref[...] + st["mix"], g2_ref[...], b2_ref[...])
        x2_ref[...] = x2
        x2b_ref[...] = x2.astype(BF16)

    def finish(ffn):
        y3_ref[...] = base_ref[...] + 0.5 * ffn

    @pl.when(jnp.logical_and(s >= 1, s <= n_tiles))
    def _():
        base_ref[...] = alpha * x2_ref[...]
        xq_ref[...] = x2b_ref[...]

        def first():
            layer_norm3()
            mixers_vpu()

        _swiglu(xq_ref[...], wg_ref, wu_ref, wd_ref, acc_ref, finish,
                other_work={0: first, 3: mixers_mxu, 4: mixers_finish, 6: out_projection,
                            7: layer_norm2})

    @pl.when(jnp.logical_or(s == 0, s == n_tiles + 1))
    def _():
        layer_norm3()
        mixers_vpu()
        mixers_mxu()
        mixers_finish()
        out_projection()
        layer_norm2()


def _mix_ffn(x1, ya, loc, local_params, w_out, g2, b2, wg, wu, wd, g3, b3, next_weights, l,
             *, alpha, tm, seq):
    n, d = x1.shape
    n_tiles = n // tm
    tiles_per_seq = seq // tm
    hb = tm // HALO
    last_hb = n // HALO - 1
    tile = lambda s: jnp.minimum(s, n_tiles - 1)
    done = lambda s: jnp.clip(s - 2, 0, n_tiles - 1)

    def main(g):
        return pl.BlockSpec((None, tm, D_GROUP), lambda s: (g, tile(s), 0))

    def prev(g):
        return pl.BlockSpec((None, HALO, D_GROUP),
                            lambda s: (g, jnp.maximum(tile(s) * hb - 1, 0), 0))

    def nxt(g):
        return pl.BlockSpec((None, HALO, D_GROUP),
                            lambda s: (g, jnp.minimum((tile(s) + 1) * hb, last_hb), 0))

    plans = [_cast_plan(w, l + 1, n_tiles) for w in next_weights]
    in_specs = ([pl.BlockSpec((tm, d), lambda s: (tile(s), 0)),
                 pl.BlockSpec((tm, D_GROUP), lambda s: (tile(s), 0)),
                 main(0), main(1), main(2), main(3), main(4), main(5),
                 prev(0), nxt(0), prev(2), nxt(2), prev(3), nxt(3)]
                + [_layer_of(w, l) for w in local_params]
                + [_resident(w_out.shape), _layer_of(g2, l), _layer_of(b2, l),
                   _resident(wg.shape), _resident(wu.shape), _resident(wd.shape),
                   _layer_of(g3, l), _layer_of(b3, l)])
    assert len(in_specs) == N_MIX_FIXED_INPUTS
    out = pl.pallas_call(
        functools.partial(_mix_ffn_body, alpha=alpha, tiles_per_seq=tiles_per_seq, seq=seq,
                          n_cast=len(plans), n_tiles=n_tiles),
        grid=(n_tiles + 2,),
        in_specs=in_specs + [p[0] for p in plans],
        out_specs=[pl.BlockSpec((tm, d), lambda s: (done(s), 0))] + [p[1] for p in plans],
        out_shape=[jax.ShapeDtypeStruct((n, d), F32)] + [p[2] for p in plans],
        scratch_shapes=[pltpu.VMEM((tm, d), F32),
                        pltpu.VMEM((tm, d), F32),
                        pltpu.VMEM((tm, d), F32),
                        pltpu.VMEM((tm, d), F32),
                        pltpu.VMEM((tm, d), BF16),
                        pltpu.VMEM((tm, d), BF16)],
        compiler_params=pltpu.CompilerParams(
            dimension_semantics=("arbitrary",), vmem_limit_bytes=V7X_VMEM_LIMIT_BYTES),
        name="mix_ffn",
    )(x1, ya, *([loc] * 12), *local_params, w_out, g2, b2, wg, wu, wd, g3, b3, *next_weights)
    return out[0], tuple(out[1:])


def kernel(x, ffn1_w_gate, ffn1_w_up, ffn1_w_down, ln1_g, ln1_b, mix_w_in, pool_w, pool_scale, conv_w, gmlp_ln_g, gmlp_ln_b, gmlp_w_s, gmlp_b_s, mix_w_out, ln2_g, ln2_b, ffn2_w_gate, ffn2_w_up, ffn2_w_down, ln3_g, ln3_b):
    bsz, seq, d = x.shape
    depth = ffn1_w_gate.shape[0]
    alpha = float((2 * depth) ** 0.25)
    tm = 512
    assert d == HEADS * D_GROUP and mix_w_in.shape[2] == N_PROJ_GROUPS * D_GROUP
    assert seq % tm == 0 and tm % CHUNK == 0 and gmlp_w_s.shape[2] == CHUNK

    wc, f1, g2 = (jnp.asarray(tbl).astype(BF16) for tbl in _dft_tables(seq))
    bf = lambda w: w.astype(BF16)
    row = lambda v: v.reshape(depth, 1, -1)

    eye = jnp.eye(HEADS, dtype=F32)
    pool_bd = bf((eye[None, :, None, :, None] * pool_w[:, :, :, None, :])
                 .reshape(depth, D_GROUP, D_GROUP))
    expand = jnp.repeat(eye, HEAD_DIM, axis=1)
    bs_full = jnp.einsum("lhq,hc->lqc", gmlp_b_s, expand, precision=lax.Precision.HIGHEST)
    local_params = (pool_bd, row(pool_scale), conv_w, row(gmlp_ln_g), row(gmlp_ln_b),
                    bf(gmlp_w_s), bs_full)

    ffn_in_f32 = (ffn1_w_gate, ffn1_w_up, ffn1_w_down, mix_w_in)
    mix_ffn_f32 = (ffn2_w_gate, ffn2_w_up, ffn2_w_down, mix_w_out)
    w1g, w1u, w1d, w_in = (bf(w[0]) for w in ffn_in_f32)

    h = x.reshape(bsz * seq, d)
    for l in range(depth):
        x1, a, loc, (w2g, w2u, w2d, w_out) = _ffn_in(
            h, w1g, w1u, w1d, row(ln1_g), row(ln1_b), w_in, mix_ffn_f32, l, alpha=alpha, tm=tm)
        ya = _fourier(a, wc, f1, g2, bsz=bsz, seq=seq)
        h, nxt = _mix_ffn(x1, ya, loc, local_params, w_out, row(ln2_g), row(ln2_b),
                          w2g, w2u, w2d, row(ln3_g), row(ln3_b),
                          ffn_in_f32 if l + 1 < depth else (), l,
                          alpha=alpha, tm=tm, seq=seq)
        if nxt:
            w1g, w1u, w1d, w_in = nxt
    return h.reshape(bsz, seq, d)
```
